```python
import jax, jax.numpy as jnp
from jax import lax
import numpy as np


D_MODEL = 4096
BATCH = 4
SEQ = 2048
DEPTH = 1

D_CONV = D_MODEL // 2
CONV_WIDTH = 31
HEAD_DIM = 128
N_HEADS = D_MODEL // (2 * HEAD_DIM)
N_KV_GROUPS = 4
HEADS_PER_GROUP = N_HEADS // N_KV_GROUPS
CMP_BLOCK = 32
CMP_STRIDE = 16
CMP_HIDDEN = 2 * HEAD_DIM
SLC_BLOCK = 64
N_SELECT = 16
WINDOW = 512
WIN_BLOCK = 128
QUERY_CHUNK = 16
N_NSA_BRANCHES = 3
D_FF = 4 * D_MODEL
D_Q = N_HEADS * HEAD_DIM
D_KV = N_KV_GROUPS * HEAD_DIM
D_IN = 2 * D_CONV + D_Q + 6 * D_KV + N_NSA_BRANCHES * N_HEADS + 2 * D_MODEL

NORM_EPS = 1e-6
LN_EPS = 1e-5
FORCED_SCORE = 1e4
MASKED_SCORE = -1e4
NEG_INF = -1e30

kernel_name = 'hybrid_conformer_nsa_gated_block'


def rms_norm(x, g):
    xf = x.astype(jnp.float32)
    y = xf * lax.rsqrt(jnp.mean(xf * xf, axis=-1, keepdims=True) + NORM_EPS)
    return (y * g.astype(jnp.float32)).astype(x.dtype)


def layer_norm(x, g, b):
    xf = x.astype(jnp.float32)
    mu = jnp.mean(xf, axis=-1, keepdims=True)
    var = jnp.mean(jnp.square(xf - mu), axis=-1, keepdims=True)
    y = (xf - mu) * lax.rsqrt(var + LN_EPS)
    return (y * g.astype(jnp.float32) + b.astype(jnp.float32)).astype(x.dtype)


def masked_softmax(s, mask):
    s = jnp.where(mask, s.astype(jnp.float32), NEG_INF)
    m = jnp.max(s, axis=-1, keepdims=True)
    e = jnp.where(mask, jnp.exp(s - m), 0.0)
    return e / jnp.maximum(jnp.sum(e, axis=-1, keepdims=True), 1e-30)


def compress_blocks(kv, pos, w1, w2):
    B, S, G, d = kv.shape
    r = CMP_BLOCK // CMP_STRIDE
    chunks = kv.reshape(B, S // CMP_STRIDE, CMP_STRIDE, G, d)
    n_cmp = S // CMP_STRIDE - r + 1
    blocks = jnp.concatenate([chunks[:, i:i + n_cmp] for i in range(r)], axis=2)
    blocks = blocks + pos[None, None, :, None, :]
    blocks = blocks.transpose(0, 1, 3, 2, 4).reshape(B, n_cmp, G, CMP_BLOCK * d)
    return jax.nn.gelu(blocks @ w1) @ w2


def gather_blocks(blocks, idx):
    return blocks[idx]


def nsa_attention(q, kc, vc, ks, vs, kw, vw, gate_logits,
                  pos_cmp_k, w_cmp_k1, w_cmp_k2, pos_cmp_v, w_cmp_v1, w_cmp_v2):
    B, S = q.shape[:2]
    G, R, d = N_KV_GROUPS, HEADS_PER_GROUP, HEAD_DIM
    scale = HEAD_DIM ** -0.5
    q5 = q.reshape(B, S, G, R, d)
    kc, vc, ks, vs, kw, vw = [a.reshape(B, S, G, d) for a in (kc, vc, ks, vs, kw, vw)]
    t = np.arange(S)

    k_cmp = compress_blocks(kc, pos_cmp_k, w_cmp_k1, w_cmp_k2)
    v_cmp = compress_blocks(vc, pos_cmp_v, w_cmp_v1, w_cmp_v2)
    n_cmp = k_cmp.shape[1]
    cmp_start = np.arange(n_cmp) * CMP_STRIDE
    cmp_vis = (cmp_start[None, :] + CMP_BLOCK - 1) <= t[:, None]
    s_cmp = jnp.einsum('bsgrd,bngd->bgrsn', q5, k_cmp) * scale
    p_cmp = masked_softmax(s_cmp, cmp_vis)
    o_cmp = jnp.einsum('bgrsn,bngd->bsgrd', p_cmp.astype(v_cmp.dtype), v_cmp)

    n_slc = S // SLC_BLOCK
    slc_start = np.arange(n_slc) * SLC_BLOCK
    overlap = ((cmp_start[:, None] < slc_start[None, :] + SLC_BLOCK)
               & (cmp_start[:, None] + CMP_BLOCK > slc_start[None, :])).astype(np.float32)
    importance = jnp.einsum('bgsn,nm->bgsm', jnp.sum(p_cmp, axis=2), overlap)
    j = np.arange(n_slc)
    cur = t // SLC_BLOCK
    valid = slc_start[None, :] <= t[:, None]
    forced = (j[None, :] == 0) | (j[None, :] == cur[:, None]) | (j[None, :] == cur[:, None] - 1)
    sel_score = jnp.where(forced & valid, FORCED_SCORE, jnp.where(valid, importance, MASKED_SCORE))
    n_sel = min(N_SELECT, n_slc)
    _, sel_idx = lax.top_k(sel_score, n_sel)

    kb = ks.reshape(B, n_slc, SLC_BLOCK, G, d).transpose(0, 3, 1, 2, 4)
    vb = vs.reshape(B, n_slc, SLC_BLOCK, G, d).transpose(0, 3, 1, 2, 4)
    nq = S // QUERY_CHUNK
    q_chunks = q5.reshape(B, nq, QUERY_CHUNK, G, R, d).transpose(1, 0, 2, 3, 4, 5)
    idx_chunks = sel_idx.reshape(B, G, nq, QUERY_CHUNK, n_sel).transpose(2, 0, 1, 3, 4)
    t_chunks = jnp.arange(S, dtype=jnp.int32).reshape(nq, QUERY_CHUNK)
    gather2 = jax.vmap(jax.vmap(gather_blocks))

    def selected_chunk(args):
        qc, ic, tc = args
        kg = gather2(kb, ic)
        vg = gather2(vb, ic)
        s = jnp.einsum('bqgrd,bgqnld->bgrqnl', qc, kg) * scale
        key_pos = ic[..., None] * SLC_BLOCK + jnp.arange(SLC_BLOCK, dtype=jnp.int32)
        mask = key_pos <= tc[:, None, None]
        qc_len = qc.shape[1]
        p = masked_softmax(s.reshape(B, G, R, qc_len, n_sel * SLC_BLOCK),
                           mask[:, :, None].reshape(B, G, 1, qc_len, n_sel * SLC_BLOCK))
        p = p.reshape(B, G, R, qc_len, n_sel, SLC_BLOCK).astype(vg.dtype)
        return jnp.einsum('bgrqnl,bgqnld->bqgrd', p, vg)

    o_slc = lax.map(selected_chunk, (q_chunks, idx_chunks, t_chunks))
    o_slc = o_slc.transpose(1, 0, 2, 3, 4, 5).reshape(B, S, G, R, d)

    nb = S // WIN_BLOCK
    lb = WINDOW // WIN_BLOCK
    pad = ((0, 0), (lb * WIN_BLOCK, 0), (0, 0), (0, 0))
    kblk = jnp.pad(kw, pad).reshape(B, nb + lb, WIN_BLOCK, G, d)
    vblk = jnp.pad(vw, pad).reshape(B, nb + lb, WIN_BLOCK, G, d)
    k_win = jnp.concatenate([kblk[:, i:i + nb] for i in range(lb + 1)], axis=2)
    v_win = jnp.concatenate([vblk[:, i:i + nb] for i in range(lb + 1)], axis=2)
    qw = q5.reshape(B, nb, WIN_BLOCK, G, R, d)
    s_win = jnp.einsum('bnqgrd,bnkgd->bgrnqk', qw, k_win) * scale
    q_pos = t.reshape(nb, WIN_BLOCK)
    k_pos = np.arange(nb)[:, None] * WIN_BLOCK - lb * WIN_BLOCK + np.arange((lb + 1) * WIN_BLOCK)[None, :]
    win_mask = (k_pos[:, None, :] <= q_pos[:, :, None]) & (k_pos[:, None, :] > q_pos[:, :, None] - WINDOW)
    p_win = masked_softmax(s_win, win_mask).astype(v_win.dtype)
    o_win = jnp.einsum('bgrnqk,bnkgd->bnqgrd', p_win, v_win).reshape(B, S, G, R, d)

    g = jax.nn.sigmoid(gate_logits.astype(jnp.float32)).reshape(B, S, G, R, N_NSA_BRANCHES).astype(q.dtype)
    o = g[..., 0:1] * o_cmp + g[..., 1:2] * o_slc + g[..., 2:3] * o_win
    return o.reshape(B, S, D_Q)


def conformer_conv(glu_in, w_dw, b_dw, ln_g, ln_b):
    a, b = jnp.split(glu_in, 2, axis=-1)
    u = a * jax.nn.sigmoid(b)
    u = lax.conv_general_dilated(u, w_dw, window_strides=(1,), padding=[(CONV_WIDTH - 1, 0)],
                                 dimension_numbers=('NWC', 'WIO', 'NWC'),
                                 feature_group_count=D_CONV) + b_dw
    u = layer_norm(u, ln_g, ln_b)
    return jax.nn.silu(u)


def hybrid_layer(x, norm_mix_pre, w_in, pos_cmp_k, w_cmp_k1, w_cmp_k2, pos_cmp_v, w_cmp_v1, w_cmp_v2,
                 w_dw, b_dw, ln_conv_g, ln_conv_b, w_conv_out, w_attn_out, w_out, norm_mix_post,
                 norm_mlp_pre, w_up, w_down, norm_mlp_post):
    u = rms_norm(x, norm_mix_pre)
    proj = u @ w_in
    sizes = [2 * D_CONV, D_Q, D_KV, D_KV, D_KV, D_KV, D_KV, D_KV,
             N_NSA_BRANCHES * N_HEADS, D_MODEL, D_MODEL]
    offsets = np.cumsum(sizes)[:-1].tolist()
    (glu_in, q, kc, vc, ks, vs, kw, vw, nsa_gates, gate_a, gate_b) = jnp.split(proj, offsets, axis=-1)

    y_conv = conformer_conv(glu_in, w_dw, b_dw, ln_conv_g, ln_conv_b) @ w_conv_out
    y_attn = nsa_attention(q, kc, vc, ks, vs, kw, vw, nsa_gates,
                           pos_cmp_k, w_cmp_k1, w_cmp_k2, pos_cmp_v, w_cmp_v1, w_cmp_v2) @ w_attn_out
    merged = jax.nn.sigmoid(gate_a) * y_conv + jax.nn.sigmoid(gate_b) * y_attn
    x = x + rms_norm(merged @ w_out, norm_mix_post)

    h = rms_norm(x, norm_mlp_pre)
    h = jnp.square(jax.nn.relu(h @ w_up)) @ w_down
    return x + rms_norm(h, norm_mlp_post)


def setup_inputs(seed: int = 0) -> dict:
    key = jax.random.key(seed)
    k = jax.random.split(key, 21)

    def nrm(kk, shape, scale):
        return jax.random.normal(kk, shape, jnp.float32) * scale

    L = DEPTH
    return {
        'x': nrm(k[0], (BATCH, SEQ, D_MODEL), 1.0),
        'norm_mix_pre': 1.0 + nrm(k[1], (L, D_MODEL), 0.05),
        'w_in': nrm(k[2], (L, D_MODEL, D_IN), D_MODEL ** -0.5),
        'pos_cmp_k': nrm(k[3], (L, CMP_BLOCK, HEAD_DIM), 0.1),
        'w_cmp_k1': nrm(k[4], (L, CMP_BLOCK * HEAD_DIM, CMP_HIDDEN), (CMP_BLOCK * HEAD_DIM) ** -0.5),
        'w_cmp_k2': nrm(k[5], (L, CMP_HIDDEN, HEAD_DIM), CMP_HIDDEN ** -0.5),
        'pos_cmp_v': nrm(k[6], (L, CMP_BLOCK, HEAD_DIM), 0.1),
        'w_cmp_v1': nrm(k[7], (L, CMP_BLOCK * HEAD_DIM, CMP_HIDDEN), (CMP_BLOCK * HEAD_DIM) ** -0.5),
        'w_cmp_v2': nrm(k[8], (L, CMP_HIDDEN, HEAD_DIM), CMP_HIDDEN ** -0.5),
        'w_dw': nrm(k[9], (L, CONV_WIDTH, 1, D_CONV), CONV_WIDTH ** -0.5),
        'b_dw': nrm(k[10], (L, D_CONV), 0.02),
        'ln_conv_g': 1.0 + nrm(k[11], (L, D_CONV), 0.05),
        'ln_conv_b': nrm(k[12], (L, D_CONV), 0.02),
        'w_conv_out': nrm(k[13], (L, D_CONV, D_MODEL), D_CONV ** -0.5),
        'w_attn_out': nrm(k[14], (L, D_Q, D_MODEL), D_Q ** -0.5),
        'w_out': nrm(k[15], (L, D_MODEL, D_MODEL), D_MODEL ** -0.5),
        'norm_mix_post': 1.0 + nrm(k[16], (L, D_MODEL), 0.05),
        'norm_mlp_pre': 1.0 + nrm(k[17], (L, D_MODEL), 0.05),
        'w_up': nrm(k[18], (L, D_MODEL, D_FF), D_MODEL ** -0.5),
        'w_down': nrm(k[19], (L, D_FF, D_MODEL), D_FF ** -0.5),
        'norm_mlp_post': 1.0 + nrm(k[20], (L, D_MODEL), 0.05),
    }


def reference(x, norm_mix_pre, w_in, pos_cmp_k, w_cmp_k1, w_cmp_k2, pos_cmp_v, w_cmp_v1, w_cmp_v2,
              w_dw, b_dw, ln_conv_g, ln_conv_b, w_conv_out, w_attn_out, w_out, norm_mix_post,
              norm_mlp_pre, w_up, w_down, norm_mlp_post):
    for l in range(DEPTH):
        x = hybrid_layer(x, norm_mix_pre[l], w_in[l], pos_cmp_k[l], w_cmp_k1[l], w_cmp_k2[l],
                         pos_cmp_v[l], w_cmp_v1[l], w_cmp_v2[l], w_dw[l], b_dw[l], ln_conv_g[l],
                         ln_conv_b[l], w_conv_out[l], w_attn_out[l], w_out[l], norm_mix_post[l],
                         norm_mlp_pre[l], w_up[l], w_down[l], norm_mlp_post[l])
    return x
```

```python
import functools

import numpy as np
import jax
import jax.numpy as jnp
from jax import lax
from jax.experimental import pallas as pl
from jax.experimental.pallas import tpu as pltpu

D_MODEL = 4096
SEQ = 2048
D_CONV = D_MODEL // 2
CONV_WIDTH = 31
HEAD_DIM = 128
N_HEADS = 16
N_KV_GROUPS = 4
HEADS_PER_GROUP = N_HEADS // N_KV_GROUPS
CMP_BLOCK = 32
CMP_STRIDE = 16
CMP_HIDDEN = 2 * HEAD_DIM
SLC_BLOCK = 64
N_SELECT = 16
WINDOW = 512
N_NSA_BRANCHES = 3
D_FF = 4 * D_MODEL
D_Q = N_HEADS * HEAD_DIM
D_KV = N_KV_GROUPS * HEAD_DIM
N_GATE = N_NSA_BRANCHES * N_HEADS
D_PROJ = 2 * D_CONV + D_Q + 6 * D_KV + 2 * D_MODEL
N_CHUNK = SEQ // CMP_STRIDE
N_CMP = N_CHUNK - CMP_BLOCK // CMP_STRIDE + 1
N_SLC = SEQ // SLC_BLOCK

NORM_EPS = 1e-6
LN_EPS = 1e-5
FORCED_SCORE = 1e4
MASKED_SCORE = -1e4
NEG_INF = -1e30

LANES = 128
HALO = 32
VMEM_LIMIT = 56 * 1024 * 1024

OFF_GLU_A = 0
OFF_GLU_B = D_CONV
OFF_Q = 2 * D_CONV
OFF_KC = OFF_Q + D_Q
OFF_VC = OFF_KC + D_KV
OFF_KS = OFF_VC + D_KV
OFF_VS = OFF_KS + D_KV
OFF_KW = OFF_VS + D_KV
OFF_VW = OFF_KW + D_KV
OFF_GATE_A = OFF_VW + D_KV
OFF_GATE_B = OFF_GATE_A + D_MODEL

F32 = jnp.float32
BF16 = jnp.bfloat16


def _params(*sem):
    return pltpu.CompilerParams(dimension_semantics=sem, vmem_limit_bytes=VMEM_LIMIT)


def _rmsnorm_kernel(x_ref, g_ref, o_ref):
    x = x_ref[...]
    y = x * lax.rsqrt(jnp.mean(x * x, axis=-1, keepdims=True) + NORM_EPS)
    o_ref[...] = (y * g_ref[...]).astype(o_ref.dtype)


def _rmsnorm(x, g, bm=256):
    m, d = x.shape
    return pl.pallas_call(
        _rmsnorm_kernel,
        grid=(m // bm,),
        in_specs=[pl.BlockSpec((bm, d), lambda i: (i, 0)), pl.BlockSpec((1, d), lambda i: (0, 0))],
        out_specs=pl.BlockSpec((bm, d), lambda i: (i, 0)),
        out_shape=jax.ShapeDtypeStruct((m, d), BF16),
        compiler_params=_params("parallel"),
        name="rmsnorm_in",
    )(x, g.reshape(1, d))


def _post_mix_kernel(x_ref, z_ref, g1_ref, g2_ref, x1_ref, h_ref):
    z = z_ref[...].astype(F32)
    zn = z * lax.rsqrt(jnp.mean(z * z, axis=-1, keepdims=True) + NORM_EPS)
    x1 = x_ref[...] + zn * g1_ref[...]
    x1_ref[...] = x1
    hn = x1 * lax.rsqrt(jnp.mean(x1 * x1, axis=-1, keepdims=True) + NORM_EPS)
    h_ref[...] = (hn * g2_ref[...]).astype(h_ref.dtype)


def _post_mix(x, z, g1, g2, bm=256):
    m, d = x.shape
    row = pl.BlockSpec((bm, d), lambda i: (i, 0))
    vec = pl.BlockSpec((1, d), lambda i: (0, 0))
    return pl.pallas_call(
        _post_mix_kernel,
        grid=(m // bm,),
        in_specs=[row, row, vec, vec],
        out_specs=[row, row],
        out_shape=[jax.ShapeDtypeStruct((m, d), F32), jax.ShapeDtypeStruct((m, d), BF16)],
        compiler_params=_params("parallel"),
        name="post_mix_norm",
    )(x, z, g1.reshape(1, d), g2.reshape(1, d))


def _post_mlp_kernel(x_ref, y_ref, g_ref, o_ref):
    y = y_ref[...].astype(F32)
    yn = y * lax.rsqrt(jnp.mean(y * y, axis=-1, keepdims=True) + NORM_EPS)
    o_ref[...] = x_ref[...] + yn * g_ref[...]


def _post_mlp(x, y, g, bm=256):
    m, d = x.shape
    row = pl.BlockSpec((bm, d), lambda i: (i, 0))
    return pl.pallas_call(
        _post_mlp_kernel,
        grid=(m // bm,),
        in_specs=[row, row, pl.BlockSpec((1, d), lambda i: (0, 0))],
        out_specs=row,
        out_shape=jax.ShapeDtypeStruct((m, d), F32),
        compiler_params=_params("parallel"),
        name="post_mlp_norm",
    )(x, y, g.reshape(1, d))


def _mm_scale_kernel(a_ref, w_ref, s_ref, o_ref):
    acc = jnp.dot(a_ref[...], w_ref[...], preferred_element_type=F32)
    o_ref[...] = (acc * s_ref[...]).astype(o_ref.dtype)


def _mm_plain_kernel(a_ref, w_ref, o_ref):
    o_ref[...] = jnp.dot(a_ref[...], w_ref[...], preferred_element_type=F32).astype(o_ref.dtype)


def _mm_relu2_kernel(a_ref, w_ref, o_ref):
    acc = jnp.dot(a_ref[...], w_ref[...], preferred_element_type=F32)
    r = jnp.maximum(acc, 0.0)
    o_ref[...] = (r * r).astype(o_ref.dtype)


def _matmul(a, w, out_dtype, *, bm, bn, name, col_scale=None, relu2=False):
    m, k = a.shape
    n = w.shape[1]
    in_specs = [pl.BlockSpec((bm, k), lambda i, j: (i, 0)), pl.BlockSpec((k, bn), lambda i, j: (0, j))]
    args = [a, w]
    if col_scale is not None:
        body = _mm_scale_kernel
        in_specs.append(pl.BlockSpec((1, bn), lambda i, j: (0, j)))
        args.append(col_scale.reshape(1, n))
    else:
        body = _mm_relu2_kernel if relu2 else _mm_plain_kernel
    return pl.pallas_call(
        body,
        grid=(m // bm, n // bn),
        in_specs=in_specs,
        out_specs=pl.BlockSpec((bm, bn), lambda i, j: (i, j)),
        out_shape=jax.ShapeDtypeStruct((m, n), out_dtype),
        compiler_params=_params("parallel", "arbitrary"),
        name=name,
    )(*args)


def _mm_kacc_kernel(a_ref, w_ref, o_ref):
    @pl.when(pl.program_id(2) == 0)
    def _():
        o_ref[...] = jnp.zeros_like(o_ref)

    o_ref[...] += jnp.dot(a_ref[...], w_ref[...], preferred_element_type=F32)


def _matmul_ktiled(a, w, *, bm, bn, bk, name):
    m, k = a.shape
    n = w.shape[1]
    return pl.pallas_call(
        _mm_kacc_kernel,
        grid=(m // bm, n // bn, k // bk),
        in_specs=[pl.BlockSpec((bm, bk), lambda i, j, kk: (i, kk)), pl.BlockSpec((bk, bn), lambda i, j, kk: (kk, j))],
        out_specs=pl.BlockSpec((bm, bn), lambda i, j, kk: (i, j)),
        out_shape=jax.ShapeDtypeStruct((m, n), F32),
        compiler_params=_params("parallel", "parallel", "arbitrary"),
        name=name,
    )(a, w)


def _merge_kernel(c_ref, a_ref, wc_ref, wa_ref, ga_ref, gb_ref, o_ref):
    yc = jnp.dot(c_ref[...], wc_ref[...], preferred_element_type=F32)
    ya = jnp.dot(a_ref[...], wa_ref[...], preferred_element_type=F32)
    ga = jax.nn.sigmoid(ga_ref[...].astype(F32))
    gb = jax.nn.sigmoid(gb_ref[...].astype(F32))
    o_ref[...] = (ga * yc + gb * ya).astype(o_ref.dtype)


def _merge(conv_act, attn_act, wc, wa, proj, *, bm=512, bn=1024):
    m, kc = conv_act.shape
    ka = attn_act.shape[1]
    n = wc.shape[1]
    ja, jb = OFF_GATE_A // bn, OFF_GATE_B // bn
    return pl.pallas_call(
        _merge_kernel,
        grid=(m // bm, n // bn),
        in_specs=[
            pl.BlockSpec((bm, kc), lambda i, j: (i, 0)),
            pl.BlockSpec((bm, ka), lambda i, j: (i, 0)),
            pl.BlockSpec((kc, bn), lambda i, j: (0, j)),
            pl.BlockSpec((ka, bn), lambda i, j: (0, j)),
            pl.BlockSpec((bm, bn), lambda i, j: (i, ja + j)),
            pl.BlockSpec((bm, bn), lambda i, j: (i, jb + j)),
        ],
        out_specs=pl.BlockSpec((bm, bn), lambda i, j: (i, j)),
        out_shape=jax.ShapeDtypeStruct((m, n), BF16),
        compiler_params=_params("parallel", "arbitrary"),
        name="gated_merge",
    )(conv_act, attn_act, wc, wa, proj, proj)


def _gelu_tanh(x):
    return 0.5 * x * (1.0 + jnp.tanh(np.sqrt(2.0 / np.pi) * (x + 0.044715 * (x * x * x))))


def _compress_kernel(xk_ref, xv_ref, pk_ref, pv_ref, w1k_ref, w1v_ref, w2k_ref, w2v_ref, ok_ref, ov_ref):
    for x_ref, p_ref, w1_ref, w2_ref, o_ref in ((xk_ref, pk_ref, w1k_ref, w2k_ref, ok_ref),
                                                (xv_ref, pv_ref, w1v_ref, w2v_ref, ov_ref)):
        x = x_ref[0, 0].astype(F32)
        first = jnp.dot((x + p_ref[0:1, :]).astype(BF16), w1_ref[0], preferred_element_type=F32)
        second = jnp.dot((x + p_ref[1:2, :]).astype(BF16), w1_ref[1], preferred_element_type=F32)
        h = first + pltpu.roll(second, N_CHUNK - 1, axis=0)
        o_ref[0, 0] = jnp.dot(_gelu_tanh(h).astype(BF16), w2_ref[...], preferred_element_type=F32).astype(o_ref.dtype)


def _compress(xk, xv, pk, pv, w1k, w1v, w2k, w2v):
    b = xk.shape[0]
    half = CMP_STRIDE * HEAD_DIM
    xspec = pl.BlockSpec((1, 1, N_CHUNK, half), lambda i, g: (i, g, 0, 0))
    pspec = pl.BlockSpec((2, half), lambda i, g: (0, 0))
    w1spec = pl.BlockSpec((2, half, CMP_HIDDEN), lambda i, g: (0, 0, 0))
    w2spec = pl.BlockSpec((CMP_HIDDEN, HEAD_DIM), lambda i, g: (0, 0))
    ospec = pl.BlockSpec((1, 1, N_CHUNK, HEAD_DIM), lambda i, g: (i, g, 0, 0))
    oshape = jax.ShapeDtypeStruct((b, N_KV_GROUPS, N_CHUNK, HEAD_DIM), BF16)
    return pl.pallas_call(
        _compress_kernel,
        grid=(b, N_KV_GROUPS),
        in_specs=[xspec, xspec, pspec, pspec, w1spec, w1spec, w2spec, w2spec],
        out_specs=[ospec, ospec],
        out_shape=[oshape, oshape],
        compiler_params=_params("parallel", "parallel"),
        name="compress_kv",
    )(xk, xv, pk, pv, w1k, w1v, w2k, w2v)


def _attn_kernel(q_ref, ks_ref, vs_ref, kw_ref, vw_ref, kc_ref, vc_ref, gl_ref, ovl_ref, o_ref,
                 m_ref, l_ref, acc_ref, *, tq, tk):
    grp = pl.program_id(1)
    qi = pl.program_id(2)
    q0 = qi * tq
    nh = HEADS_PER_GROUP
    rows = nh * tq
    nt = (((1,), (1,)), ((), ()))

    q = q_ref[0]
    qs = jnp.concatenate([q[:, r * HEAD_DIM:(r + 1) * HEAD_DIM] for r in range(nh)], axis=0)

    s = lax.dot_general(qs, kc_ref[0, 0], nt, preferred_element_type=F32).reshape(nh, tq, N_CHUNK)
    n_idx = lax.broadcasted_iota(jnp.int32, (tq, N_CHUNK), 1)
    t_idx = q0 + lax.broadcasted_iota(jnp.int32, (tq, N_CHUNK), 0)
    vis = ((n_idx * CMP_STRIDE + (CMP_BLOCK - 1) <= t_idx) & (n_idx < N_CMP))[None]
    sm = jnp.where(vis, s, NEG_INF)
    mx = jnp.max(sm, axis=-1, keepdims=True)
    e = jnp.where(vis, jnp.exp(sm - mx), 0.0)
    p = e / jnp.maximum(jnp.sum(e, axis=-1, keepdims=True), 1e-30)
    o_cmp = jnp.dot(p.reshape(rows, N_CHUNK).astype(BF16), vc_ref[0, 0], preferred_element_type=F32)

    p_sum = p[0] + p[1] + p[2] + p[3]
    p_hi = p_sum.astype(BF16)
    rem = p_sum - p_hi.astype(F32)
    p_mid = rem.astype(BF16)
    p_lo = (rem - p_mid.astype(F32)).astype(BF16)
    ovl = ovl_ref[...]
    imp = (lax.dot_general(ovl, p_hi, nt, preferred_element_type=F32)
           + lax.dot_general(ovl, p_mid, nt, preferred_element_type=F32)
           + lax.dot_general(ovl, p_lo, nt, preferred_element_type=F32))

    j_idx = lax.broadcasted_iota(jnp.int32, (LANES, tq), 0)
    cur = jnp.right_shift(q0 + lax.broadcasted_iota(jnp.int32, (LANES, tq), 1), int(np.log2(SLC_BLOCK)))
    valid = j_idx <= cur
    forced = (j_idx == 0) | (j_idx == cur) | (j_idx == cur - 1)
    score = jnp.where(forced & valid, FORCED_SCORE, jnp.where(valid, imp, MASKED_SCORE))
    rank = jnp.zeros((LANES, tq), jnp.int32)
    for k in range(N_SLC):
        sk = score[k:k + 1, :]
        beats = (sk > score) | ((sk == score) & (j_idx > k))
        rank = rank + beats.astype(jnp.int32)
    sel_t = jnp.where((rank < N_SELECT) & (j_idx < N_SLC), 1.0, 0.0)
    sel = jnp.transpose(sel_t).astype(BF16)

    row_t = q0 + lax.broadcasted_iota(jnp.int32, (tq, tk), 0)
    col_k = lax.broadcasted_iota(jnp.int32, (tq, tk), 1)

    def flash_init():
        m_ref[...] = jnp.full(m_ref.shape, NEG_INF, F32)
        l_ref[...] = jnp.zeros(l_ref.shape, F32)
        acc_ref[...] = jnp.zeros(acc_ref.shape, F32)

    def flash_step(k_ref, v_ref, kt, bias):
        k0 = pl.multiple_of(kt * tk, tk)
        k = k_ref[0, pl.ds(k0, tk), :]
        v = v_ref[0, pl.ds(k0, tk), :]
        s3 = lax.dot_general(qs, k, nt, preferred_element_type=F32).reshape(nh, tq, tk) + bias[None]
        m_old = m_ref[...]
        m_new = jnp.maximum(m_old, jnp.max(s3, axis=-1, keepdims=True))
        alpha = jnp.exp(m_old - m_new)
        pr = jnp.exp(s3 - m_new)
        l_ref[...] = alpha * l_ref[...] + jnp.sum(pr, axis=-1, keepdims=True)
        m_ref[...] = m_new
        pv = jnp.dot(pr.reshape(rows, tk).astype(BF16), v, preferred_element_type=F32)
        acc_ref[...] = alpha.reshape(rows, 1) * acc_ref[...] + pv

    def flash_result():
        return acc_ref[...] / l_ref[...].reshape(rows, 1)

    jk = lax.broadcasted_iota(jnp.int32, (LANES, tk), 0)
    kk = jnp.right_shift(lax.broadcasted_iota(jnp.int32, (LANES, tk), 1), int(np.log2(SLC_BLOCK)))

    def slc_body(kt, carry):
        expand = jnp.where(jk == kt * (tk // SLC_BLOCK) + kk, 1.0, 0.0).astype(BF16)
        chosen = jnp.dot(sel, expand, preferred_element_type=F32)
        ok = (chosen > 0.5) & (kt * tk + col_k <= row_t)
        flash_step(ks_ref, vs_ref, kt, jnp.where(ok, 0.0, NEG_INF))
        return carry

    flash_init()
    lax.fori_loop(0, (q0 + tq + tk - 1) // tk, slc_body, 0)
    o_slc = flash_result()

    n_back = (WINDOW + tk - 1) // tk

    def win_body(d, carry):
        kt = (q0 + tq - 1) // tk - d
        key = kt * tk + col_k
        ok = (key <= row_t) & (key > row_t - WINDOW)
        flash_step(kw_ref, vw_ref, kt, jnp.where(ok, 0.0, NEG_INF))
        return carry

    flash_init()
    lax.fori_loop(0, jnp.minimum((q0 + tq - 1) // tk, n_back) + 1, win_body, 0)
    n_pad = jnp.maximum(WINDOW - 1 - (q0 + lax.broadcasted_iota(jnp.int32, (tq, 1), 0)), 0).astype(F32)[None]
    m_old = m_ref[...]
    m_new = jnp.where(n_pad > 0.0, jnp.maximum(m_old, 0.0), m_old)
    alpha = jnp.exp(m_old - m_new)
    l_ref[...] = alpha * l_ref[...] + n_pad * jnp.exp(-m_new)
    acc_ref[...] = alpha.reshape(rows, 1) * acc_ref[...]
    o_win = flash_result()

    gates = jax.nn.sigmoid(gl_ref[0])
    gates = pltpu.roll(gates, (grp * (LANES - nh * N_NSA_BRANCHES)) % LANES, axis=1)
    for r in range(nh):
        sl = slice(r * tq, (r + 1) * tq)
        c = r * N_NSA_BRANCHES
        o_r = (gates[:, c:c + 1] * o_cmp[sl] + gates[:, c + 1:c + 2] * o_slc[sl] + gates[:, c + 2:c + 3] * o_win[sl])
        o_ref[0, :, r * HEAD_DIM:(r + 1) * HEAD_DIM] = o_r.astype(o_ref.dtype)


def _attention(proj3, gate_logits3, k_cmp, v_cmp, overlap_t, *, tq=256, tk=256):
    b = proj3.shape[0]
    gw = HEADS_PER_GROUP * HEAD_DIM
    kv = lambda off: pl.BlockSpec((1, SEQ, HEAD_DIM), lambda i, g, t: (i, 0, off // HEAD_DIM + g))
    cmp_spec = pl.BlockSpec((1, 1, N_CHUNK, HEAD_DIM), lambda i, g, t: (i, g, 0, 0))
    rows = HEADS_PER_GROUP * tq
    return pl.pallas_call(
        functools.partial(_attn_kernel, tq=tq, tk=tk),
        grid=(b, N_KV_GROUPS, SEQ // tq),
        in_specs=[
            pl.BlockSpec((1, tq, gw), lambda i, g, t: (i, t, OFF_Q // gw + g)),
            kv(OFF_KS), kv(OFF_VS), kv(OFF_KW), kv(OFF_VW),
            cmp_spec, cmp_spec,
            pl.BlockSpec((1, tq, LANES), lambda i, g, t: (i, t, 0)),
            pl.BlockSpec((LANES, N_CHUNK), lambda i, g, t: (0, 0)),
        ],
        out_specs=pl.BlockSpec((1, tq, gw), lambda i, g, t: (i, t, g)),
        out_shape=jax.ShapeDtypeStruct((b, SEQ, D_Q), BF16),
        scratch_shapes=[
            pltpu.VMEM((HEADS_PER_GROUP, tq, 1), F32),
            pltpu.VMEM((HEADS_PER_GROUP, tq, 1), F32),
            pltpu.VMEM((rows, HEAD_DIM), F32),
        ],
        compiler_params=_params("parallel", "parallel", "arbitrary"),
        name="nsa_attention",
    )(proj3, proj3, proj3, proj3, proj3, k_cmp, v_cmp, gate_logits3, overlap_t)


def _conv_kernel(a_ref, b_ref, ah_ref, bh_ref, w_ref, bias_ref, lg_ref, lb_ref, o_ref, u_ref, y_ref, *, ts):
    nc = D_CONV // LANES
    first = pl.program_id(1) == 0
    u_main = a_ref[0].astype(F32) * jax.nn.sigmoid(b_ref[0].astype(F32))
    u_halo = ah_ref[0].astype(F32) * jax.nn.sigmoid(bh_ref[0].astype(F32))
    u_halo = jnp.where(first, 0.0, u_halo)
    for c in range(nc):
        u_ref[c, 0:HALO, :] = u_halo[:, c * LANES:(c + 1) * LANES]
        u_ref[c, HALO:HALO + ts, :] = u_main[:, c * LANES:(c + 1) * LANES]

    rc = 64
    base = HALO - (CONV_WIDTH - 1)

    def chunk_body(c, carry):
        for r0 in range(0, ts, rc):
            acc = jnp.zeros((rc, LANES), F32)
            for j in range(CONV_WIDTH):
                acc = acc + u_ref[c, pl.ds(base + r0 + j, rc), :] * w_ref[c, j:j + 1, :]
            y_ref[c, r0:r0 + rc, :] = acc
        return carry

    lax.fori_loop(0, nc, chunk_body, 0)

    y = jnp.concatenate([y_ref[c] for c in range(nc)], axis=1) + bias_ref[...]
    mu = jnp.mean(y, axis=-1, keepdims=True)
    d = y - mu
    var = jnp.mean(d * d, axis=-1, keepdims=True)
    z = d * lax.rsqrt(var + LN_EPS) * lg_ref[...] + lb_ref[...]
    o_ref[0] = (z * jax.nn.sigmoid(z)).astype(o_ref.dtype)


def _conformer_conv(proj3, w_chunks, b_dw, ln_g, ln_b, *, ts=256):
    b = proj3.shape[0]
    nc = D_CONV // LANES
    per = ts // HALO
    main = lambda off: pl.BlockSpec((1, ts, D_CONV), lambda i, t: (i, t, off // D_CONV))
    halo = lambda off: pl.BlockSpec((1, HALO, D_CONV), lambda i, t: (i, jnp.maximum(t * per - 1, 0), off // D_CONV))
    vec = pl.BlockSpec((1, D_CONV), lambda i, t: (0, 0))
    return pl.pallas_call(
        functools.partial(_conv_kernel, ts=ts),
        grid=(b, SEQ // ts),
        in_specs=[main(OFF_GLU_A), main(OFF_GLU_B), halo(OFF_GLU_A), halo(OFF_GLU_B),
                  pl.BlockSpec((nc, HALO, LANES), lambda i, t: (0, 0, 0)), vec, vec, vec],
        out_specs=pl.BlockSpec((1, ts, D_CONV), lambda i, t: (i, t, 0)),
        out_shape=jax.ShapeDtypeStruct((b, SEQ, D_CONV), BF16),
        scratch_shapes=[pltpu.VMEM((nc, HALO + ts, LANES), F32), pltpu.VMEM((nc, ts, LANES), F32)],
        compiler_params=_params("parallel", "arbitrary"),
        name="conformer_conv",
    )(proj3, proj3, proj3, proj3, w_chunks, b_dw.reshape(1, D_CONV), ln_g.reshape(1, D_CONV), ln_b.reshape(1, D_CONV))


def _overlap_t():
    cmp_start = np.arange(N_CHUNK) * CMP_STRIDE
    slc_start = np.arange(LANES) * SLC_BLOCK
    ov = ((cmp_start[None, :] < slc_start[:, None] + SLC_BLOCK) & (cmp_start[None, :] + CMP_BLOCK > slc_start[:, None])
          & (np.arange(N_CHUNK)[None, :] < N_CMP) & (np.arange(LANES)[:, None] < N_SLC))
    return jnp.asarray(ov.astype(np.float32), dtype=BF16)


def _layer(x, norm_mix_pre, w_in, pos_cmp_k, w_cmp_k1, w_cmp_k2, pos_cmp_v, w_cmp_v1, w_cmp_v2,
           w_dw, b_dw, ln_conv_g, ln_conv_b, w_conv_out, w_attn_out, w_out, norm_mix_post,
           norm_mlp_pre, w_up, w_down, norm_mlp_post):
    b, s, d = x.shape
    m = b * s
    x2 = x.reshape(m, d)

    w_main = jnp.concatenate([w_in[:, :OFF_GATE_A], w_in[:, OFF_GATE_A + N_GATE:]], axis=1).astype(BF16)
    w_gate = jnp.pad(w_in[:, OFF_GATE_A:OFF_GATE_A + N_GATE], ((0, 0), (0, LANES - N_GATE))).astype(BF16)
    col_scale = jnp.ones((D_PROJ,), F32).at[OFF_Q:OFF_Q + D_Q].set(HEAD_DIM ** -0.5)
    half = CMP_STRIDE * HEAD_DIM

    u = _rmsnorm(x2, norm_mix_pre)
    proj = _matmul(u, w_main, BF16, bm=1024, bn=1024, name="in_proj", col_scale=col_scale)
    gate_logits = _matmul(u, w_gate, F32, bm=1024, bn=LANES, name="in_proj_gates")
    proj3 = proj.reshape(b, s, D_PROJ)

    def chunks(off):
        t = proj3[:, :, off:off + D_KV].reshape(b, N_CHUNK, CMP_STRIDE, N_KV_GROUPS, HEAD_DIM)
        return t.transpose(0, 3, 1, 2, 4).reshape(b, N_KV_GROUPS, N_CHUNK, half)

    k_cmp, v_cmp = _compress(
        chunks(OFF_KC), chunks(OFF_VC), pos_cmp_k.reshape(2, half), pos_cmp_v.reshape(2, half),
        w_cmp_k1.reshape(2, half, CMP_HIDDEN).astype(BF16), w_cmp_v1.reshape(2, half, CMP_HIDDEN).astype(BF16),
        w_cmp_k2.astype(BF16), w_cmp_v2.astype(BF16))

    attn = _attention(proj3, gate_logits.reshape(b, s, LANES), k_cmp, v_cmp, _overlap_t())

    w_chunks = jnp.pad(w_dw.reshape(CONV_WIDTH, D_CONV), ((0, HALO - CONV_WIDTH), (0, 0)))
    w_chunks = w_chunks.reshape(HALO, D_CONV // LANES, LANES).transpose(1, 0, 2)
    conv = _conformer_conv(proj3, w_chunks, b_dw, ln_conv_g, ln_conv_b)

    merged = _merge(conv.reshape(m, D_CONV), attn.reshape(m, D_Q), w_conv_out.astype(BF16), w_attn_out.astype(BF16), proj)
    z = _matmul(merged, w_out.astype(BF16), F32, bm=1024, bn=1024, name="out_proj")
    x1, h = _post_mix(x2, z, norm_mix_post, norm_mlp_pre)

    hidden = _matmul(h, w_up.astype(BF16), BF16, bm=1024, bn=1024, name="mlp_up", relu2=True)
    y = _matmul_ktiled(hidden, w_down.astype(BF16), bm=1024, bn=1024, bk=4096, name="mlp_down")
    return _post_mlp(x1, y, norm_mlp_post).reshape(b, s, d)


def kernel(x, norm_mix_pre, w_in, pos_cmp_k, w_cmp_k1, w_cmp_k2, pos_cmp_v, w_cmp_v1, w_cmp_v2, w_dw, b_dw,
           ln_conv_g, ln_conv_b, w_conv_out, w_attn_out, w_out, norm_mix_post, norm_mlp_pre, w_up, w_down,
           norm_mlp_post):
    for l in range(norm_mix_pre.shape[0]):
        x = _layer(x, norm_mix_pre[l], w_in[l], pos_cmp_k[l], w_cmp_k1[l], w_cmp_k2[l], pos_cmp_v[l], w_cmp_v1[l],
                   w_cmp_v2[l], w_dw[l], b_dw[l], ln_conv_g[l], ln_conv_b[l], w_conv_out[l], w_attn_out[l], w_out[l],
                   norm_mix_post[l], norm_mlp_pre[l], w_up[l], w_down[l], norm_mlp_post[l])
    return x
```

```python
import functools

import numpy as np
import jax
import jax.numpy as jnp
from jax import lax
from jax.experimental import pallas as pl
from jax.experimental.pallas import tpu as pltpu

D_MODEL = 4096
SEQ = 2048
D_CONV = D_MODEL // 2
CONV_WIDTH = 31
HEAD_DIM = 128
N_HEADS = 16
N_KV_GROUPS = 4
HEADS_PER_GROUP = N_HEADS // N_KV_GROUPS
CMP_BLOCK = 32
CMP_STRIDE = 16
CMP_HIDDEN = 2 * HEAD_DIM
SLC_BLOCK = 64
N_SELECT = 16
WINDOW = 512
N_NSA_BRANCHES = 3
D_FF = 4 * D_MODEL
D_Q = N_HEADS * HEAD_DIM
D_KV = N_KV_GROUPS * HEAD_DIM
N_GATE = N_NSA_BRANCHES * N_HEADS
D_PROJ = 2 * D_CONV + D_Q + 6 * D_KV + 2 * D_MODEL
N_CHUNK = SEQ // CMP_STRIDE
N_CMP = N_CHUNK - CMP_BLOCK // CMP_STRIDE + 1
N_SLC = SEQ // SLC_BLOCK

NORM_EPS = 1e-6
LN_EPS = 1e-5
FORCED_SCORE = 1e4
MASKED_SCORE = -1e4
NEG_INF = -1e30
MASK_BIG = 2.0 ** 100
LOG2_E = float(np.log2(np.e))

LANES = 128
HALO = 32
ATTN_TILE = 256
VMEM_LIMIT = 56 * 1024 * 1024

OFF_GLU_A = 0
OFF_GLU_B = D_CONV
OFF_Q = 2 * D_CONV
OFF_KC = OFF_Q + D_Q
OFF_VC = OFF_KC + D_KV
OFF_KS = OFF_VC + D_KV
OFF_VS = OFF_KS + D_KV
OFF_KW = OFF_VS + D_KV
OFF_VW = OFF_KW + D_KV
OFF_GATE_A = OFF_VW + D_KV
OFF_GATE_B = OFF_GATE_A + D_MODEL

F32 = jnp.float32
BF16 = jnp.bfloat16


def _params(*sem):
    return pltpu.CompilerParams(dimension_semantics=sem, vmem_limit_bytes=VMEM_LIMIT)


def _rmsnorm_kernel(x_ref, g_ref, o_ref):
    x = x_ref[...]
    y = x * lax.rsqrt(jnp.mean(x * x, axis=-1, keepdims=True) + NORM_EPS)
    o_ref[...] = (y * g_ref[...]).astype(o_ref.dtype)


def _rmsnorm(x, g, bm=256):
    m, d = x.shape
    return pl.pallas_call(
        _rmsnorm_kernel,
        grid=(m // bm,),
        in_specs=[pl.BlockSpec((bm, d), lambda i: (i, 0)), pl.BlockSpec((1, d), lambda i: (0, 0))],
        out_specs=pl.BlockSpec((bm, d), lambda i: (i, 0)),
        out_shape=jax.ShapeDtypeStruct((m, d), BF16),
        compiler_params=_params("parallel"),
        name="rmsnorm_in",
    )(x, g.reshape(1, d))


def _post_mix_kernel(x_ref, z_ref, g1_ref, g2_ref, x1_ref, h_ref):
    z = z_ref[...].astype(F32)
    zn = z * lax.rsqrt(jnp.mean(z * z, axis=-1, keepdims=True) + NORM_EPS)
    x1 = x_ref[...] + zn * g1_ref[...]
    x1_ref[...] = x1
    hn = x1 * lax.rsqrt(jnp.mean(x1 * x1, axis=-1, keepdims=True) + NORM_EPS)
    h_ref[...] = (hn * g2_ref[...]).astype(h_ref.dtype)


def _post_mix(x, z, g1, g2, bm=256):
    m, d = x.shape
    row = pl.BlockSpec((bm, d), lambda i: (i, 0))
    vec = pl.BlockSpec((1, d), lambda i: (0, 0))
    return pl.pallas_call(
        _post_mix_kernel,
        grid=(m // bm,),
        in_specs=[row, row, vec, vec],
        out_specs=[row, row],
        out_shape=[jax.ShapeDtypeStruct((m, d), F32), jax.ShapeDtypeStruct((m, d), BF16)],
        compiler_params=_params("parallel"),
        name="post_mix_norm",
    )(x, z, g1.reshape(1, d), g2.reshape(1, d))


def _post_mlp_kernel(x_ref, y_ref, g_ref, o_ref):
    y = y_ref[...].astype(F32)
    yn = y * lax.rsqrt(jnp.mean(y * y, axis=-1, keepdims=True) + NORM_EPS)
    o_ref[...] = x_ref[...] + yn * g_ref[...]


def _post_mlp(x, y, g, bm=256):
    m, d = x.shape
    row = pl.BlockSpec((bm, d), lambda i: (i, 0))
    return pl.pallas_call(
        _post_mlp_kernel,
        grid=(m // bm,),
        in_specs=[row, row, pl.BlockSpec((1, d), lambda i: (0, 0))],
        out_specs=row,
        out_shape=jax.ShapeDtypeStruct((m, d), F32),
        compiler_params=_params("parallel"),
        name="post_mlp_norm",
    )(x, y, g.reshape(1, d))


def _mm_scale_kernel(a_ref, w_ref, s_ref, o_ref):
    acc = jnp.dot(a_ref[...], w_ref[...], preferred_element_type=F32)
    o_ref[...] = (acc * s_ref[...]).astype(o_ref.dtype)


def _mm_plain_kernel(a_ref, w_ref, o_ref):
    o_ref[...] = jnp.dot(a_ref[...], w_ref[...], preferred_element_type=F32).astype(o_ref.dtype)


def _mm_relu2_kernel(a_ref, w_ref, o_ref):
    acc = jnp.dot(a_ref[...], w_ref[...], preferred_element_type=F32)
    r = jnp.maximum(acc, 0.0)
    o_ref[...] = (r * r).astype(o_ref.dtype)


def _matmul(a, w, out_dtype, *, bm, bn, name, col_scale=None, relu2=False):
    m, k = a.shape
    n = w.shape[1]
    in_specs = [pl.BlockSpec((bm, k), lambda i, j: (i, 0)), pl.BlockSpec((k, bn), lambda i, j: (0, j))]
    args = [a, w]
    if col_scale is not None:
        body = _mm_scale_kernel
        in_specs.append(pl.BlockSpec((1, bn), lambda i, j: (0, j)))
        args.append(col_scale.reshape(1, n))
    else:
        body = _mm_relu2_kernel if relu2 else _mm_plain_kernel
    return pl.pallas_call(
        body,
        grid=(m // bm, n // bn),
        in_specs=in_specs,
        out_specs=pl.BlockSpec((bm, bn), lambda i, j: (i, j)),
        out_shape=jax.ShapeDtypeStruct((m, n), out_dtype),
        compiler_params=_params("parallel", "arbitrary"),
        name=name,
    )(*args)


def _mm_kacc_kernel(a_ref, w_ref, o_ref):
    @pl.when(pl.program_id(2) == 0)
    def _():
        o_ref[...] = jnp.zeros_like(o_ref)

    o_ref[...] += jnp.dot(a_ref[...], w_ref[...], preferred_element_type=F32)


def _matmul_ktiled(a, w, *, bm, bn, bk, name):
    m, k = a.shape
    n = w.shape[1]
    return pl.pallas_call(
        _mm_kacc_kernel,
        grid=(m // bm, n // bn, k // bk),
        in_specs=[pl.BlockSpec((bm, bk), lambda i, j, kk: (i, kk)), pl.BlockSpec((bk, bn), lambda i, j, kk: (kk, j))],
        out_specs=pl.BlockSpec((bm, bn), lambda i, j, kk: (i, j)),
        out_shape=jax.ShapeDtypeStruct((m, n), F32),
        compiler_params=_params("parallel", "parallel", "arbitrary"),
        name=name,
    )(a, w)


def _merge_kernel(c_ref, a_ref, wc_ref, wa_ref, ga_ref, gb_ref, o_ref):
    yc = jnp.dot(c_ref[...], wc_ref[...], preferred_element_type=F32)
    ya = jnp.dot(a_ref[...], wa_ref[...], preferred_element_type=F32)
    ga = jax.nn.sigmoid(ga_ref[...].astype(F32))
    gb = jax.nn.sigmoid(gb_ref[...].astype(F32))
    o_ref[...] = (ga * yc + gb * ya).astype(o_ref.dtype)


def _merge(conv_act, attn_act, wc, wa, proj, *, bm=512, bn=1024):
    m, kc = conv_act.shape
    ka = attn_act.shape[1]
    n = wc.shape[1]
    ja, jb = OFF_GATE_A // bn, OFF_GATE_B // bn
    return pl.pallas_call(
        _merge_kernel,
        grid=(m // bm, n // bn),
        in_specs=[
            pl.BlockSpec((bm, kc), lambda i, j: (i, 0)),
            pl.BlockSpec((bm, ka), lambda i, j: (i, 0)),
            pl.BlockSpec((kc, bn), lambda i, j: (0, j)),
            pl.BlockSpec((ka, bn), lambda i, j: (0, j)),
            pl.BlockSpec((bm, bn), lambda i, j: (i, ja + j)),
            pl.BlockSpec((bm, bn), lambda i, j: (i, jb + j)),
        ],
        out_specs=pl.BlockSpec((bm, bn), lambda i, j: (i, j)),
        out_shape=jax.ShapeDtypeStruct((m, n), BF16),
        compiler_params=_params("parallel", "arbitrary"),
        name="gated_merge",
    )(conv_act, attn_act, wc, wa, proj, proj)


def _gelu_tanh(x):
    return 0.5 * x * (1.0 + jnp.tanh(np.sqrt(2.0 / np.pi) * (x + 0.044715 * (x * x * x))))


def _compress_kernel(xk_ref, xv_ref, pk_ref, pv_ref, w1k_ref, w1v_ref, w2k_ref, w2v_ref, ok_ref, ov_ref):
    for x_ref, p_ref, w1_ref, w2_ref, o_ref in ((xk_ref, pk_ref, w1k_ref, w2k_ref, ok_ref),
                                                (xv_ref, pv_ref, w1v_ref, w2v_ref, ov_ref)):
        x = x_ref[0, 0].astype(F32)
        first = jnp.dot((x + p_ref[0:1, :]).astype(BF16), w1_ref[0], preferred_element_type=F32)
        second = jnp.dot((x + p_ref[1:2, :]).astype(BF16), w1_ref[1], preferred_element_type=F32)
        h = first + pltpu.roll(second, N_CHUNK - 1, axis=0)
        o_ref[0, 0] = jnp.dot(_gelu_tanh(h).astype(BF16), w2_ref[...], preferred_element_type=F32).astype(o_ref.dtype)


def _compress(xk, xv, pk, pv, w1k, w1v, w2k, w2v):
    b = xk.shape[0]
    half = CMP_STRIDE * HEAD_DIM
    xspec = pl.BlockSpec((1, 1, N_CHUNK, half), lambda i, g: (i, g, 0, 0))
    pspec = pl.BlockSpec((2, half), lambda i, g: (0, 0))
    w1spec = pl.BlockSpec((2, half, CMP_HIDDEN), lambda i, g: (0, 0, 0))
    w2spec = pl.BlockSpec((CMP_HIDDEN, HEAD_DIM), lambda i, g: (0, 0))
    ospec = pl.BlockSpec((1, 1, N_CHUNK, HEAD_DIM), lambda i, g: (i, g, 0, 0))
    oshape = jax.ShapeDtypeStruct((b, N_KV_GROUPS, N_CHUNK, HEAD_DIM), BF16)
    return pl.pallas_call(
        _compress_kernel,
        grid=(b, N_KV_GROUPS),
        in_specs=[xspec, xspec, pspec, pspec, w1spec, w1spec, w2spec, w2spec],
        out_specs=[ospec, ospec],
        out_shape=[oshape, oshape],
        compiler_params=_params("parallel", "parallel"),
        name="compress_kv",
    )(xk, xv, pk, pv, w1k, w1v, w2k, w2v)


def _attn_kernel(q_ref, ks_ref, vs_ref, kw_ref, vw_ref, kc_ref, vc_ref, gl_ref, ovl_ref, blk_ref, o_ref,
                 s_ref, mx_ref, acc_ref, *, tq):
    tk = tq
    grp = pl.program_id(1)
    qi = pl.program_id(2)
    q0 = qi * tq
    nh = HEADS_PER_GROUP
    rows = nh * tq
    nt = (((1,), (1,)), ((), ()))

    q = q_ref[0]
    qs = jnp.concatenate([q[:, r * HEAD_DIM:(r + 1) * HEAD_DIM] for r in range(nh)], axis=0)

    s = lax.dot_general(qs, kc_ref[0, 0], nt, preferred_element_type=F32).reshape(nh, tq, N_CHUNK)
    n_idx = lax.broadcasted_iota(jnp.int32, (tq, N_CHUNK), 1)
    t_idx = q0 + lax.broadcasted_iota(jnp.int32, (tq, N_CHUNK), 0)
    vis = ((n_idx * CMP_STRIDE + (CMP_BLOCK - 1) <= t_idx) & (n_idx < N_CMP))[None]
    sm = jnp.where(vis, s, NEG_INF)
    mx = jnp.maximum(jnp.max(sm, axis=-1, keepdims=True), 0.5 * NEG_INF)
    e = jnp.exp2(sm - mx)
    p = e * (1.0 / jnp.maximum(jnp.sum(e, axis=-1, keepdims=True), 1e-30))
    o_cmp = jnp.dot(p.reshape(rows, N_CHUNK).astype(BF16), vc_ref[0, 0], preferred_element_type=F32)

    p_sum = p[0] + p[1] + p[2] + p[3]
    p_hi = p_sum.astype(BF16)
    rem = p_sum - p_hi.astype(F32)
    p_mid = rem.astype(BF16)
    p_lo = (rem - p_mid.astype(F32)).astype(BF16)
    ovl = ovl_ref[...]
    imp = (lax.dot_general(ovl, p_hi, nt, preferred_element_type=F32)
           + lax.dot_general(ovl, p_mid, nt, preferred_element_type=F32)
           + lax.dot_general(ovl, p_lo, nt, preferred_element_type=F32))

    j_idx = lax.broadcasted_iota(jnp.int32, (N_SLC, tq), 0)
    cur = jnp.right_shift(q0 + lax.broadcasted_iota(jnp.int32, (N_SLC, tq), 1), int(np.log2(SLC_BLOCK)))
    valid = j_idx <= cur
    forced = (j_idx == 0) | (j_idx == cur) | (j_idx == cur - 1)
    score = jnp.where(forced & valid, FORCED_SCORE, jnp.where(valid, imp, MASKED_SCORE))
    rank = jnp.zeros((N_SLC, tq), jnp.int32)
    for k in range(N_SLC):
        sk = score[k:k + 1, :]
        beats = (sk > score) | ((sk == score) & (j_idx > k))
        rank = rank + beats.astype(jnp.int32)
    unsel_t = jnp.where(rank < N_SELECT, 0.0, 1.0)
    unsel_t = jnp.concatenate([unsel_t, jnp.zeros((LANES - N_SLC, tq), F32)], axis=0)
    unsel = jnp.transpose(unsel_t).astype(BF16)

    q_slc = jnp.concatenate([qs, jnp.concatenate([unsel] * nh, axis=0)], axis=1)

    local_q = lax.broadcasted_iota(jnp.int32, (tq, tk), 0)
    local_k = lax.broadcasted_iota(jnp.int32, (tq, tk), 1)
    causal_bias = jnp.where(local_k <= local_q, 0.0, NEG_INF)
    far_bias = jnp.where(local_k > local_q, 0.0, NEG_INF)
    ones_cols = jnp.ones((tk, LANES), BF16)

    def start_branch():
        mx_ref[...] = jnp.full(mx_ref.shape, NEG_INF, F32)
        acc_ref[...] = jnp.zeros(acc_ref.shape, F32)

    def pass1(qa, k_ref, with_blocks, kt, slot, bias):
        k0 = pl.multiple_of(kt * tk, tk)
        k = k_ref[0, pl.ds(k0, tk), :]
        if with_blocks:
            k = jnp.concatenate([k, blk_ref[pl.ds(k0, tk), :]], axis=1)
        sc = lax.dot_general(qa, k, nt, preferred_element_type=F32)
        if bias is not None:
            sc = (sc.reshape(nh, tq, tk) + bias[None]).reshape(rows, tk)
        s_ref[slot] = sc
        mx_ref[...] = jnp.maximum(mx_ref[...], jnp.maximum(sc[:, :LANES], sc[:, LANES:]))

    def row_max():
        return jnp.max(mx_ref[...], axis=-1, keepdims=True)

    def pass2(v_ref, kt, slot):
        k0 = pl.multiple_of(kt * tk, tk)
        v_ext = jnp.concatenate([v_ref[0, pl.ds(k0, tk), :], ones_cols], axis=1)
        m_rep = mx_ref[...]
        pr = jnp.exp2(s_ref[slot] - jnp.concatenate([m_rep, m_rep], axis=1))
        acc_ref[...] += jnp.dot(pr.astype(BF16), v_ext, preferred_element_type=F32)

    start_branch()

    def slc_pass1(kt, carry):
        pass1(q_slc, ks_ref, True, kt, kt, None)
        return carry

    lax.fori_loop(0, qi, slc_pass1, 0)
    pass1(q_slc, ks_ref, True, qi, qi, causal_bias)
    mx_ref[...] = jnp.broadcast_to(row_max(), (rows, LANES))

    def slc_pass2(kt, carry):
        pass2(vs_ref, kt, kt)
        return carry

    lax.fori_loop(0, qi + 1, slc_pass2, 0)
    acc = acc_ref[...]
    o_slc = acc[:, :HEAD_DIM] * (1.0 / acc[:, HEAD_DIM:HEAD_DIM + 1])

    start_branch()
    pass1(qs, kw_ref, False, qi, 0, causal_bias)

    @pl.when(qi >= 1)
    def _():
        pass1(qs, kw_ref, False, qi - 1, 1, None)

    @pl.when(qi >= 2)
    def _():
        pass1(qs, kw_ref, False, qi - 2, 2, far_bias)

    n_pad = jnp.maximum(WINDOW - 1 - (q0 + lax.broadcasted_iota(jnp.int32, (tq, 1), 0)), 0).astype(F32)
    n_pad = jnp.concatenate([n_pad] * nh, axis=0)
    m_win = row_max()
    m_win = jnp.where(n_pad > 0.0, jnp.maximum(m_win, 0.0), m_win)
    mx_ref[...] = jnp.broadcast_to(m_win, (rows, LANES))
    pass2(vw_ref, qi, 0)

    @pl.when(qi >= 1)
    def _():
        pass2(vw_ref, qi - 1, 1)

    @pl.when(qi >= 2)
    def _():
        pass2(vw_ref, qi - 2, 2)

    acc = acc_ref[...]
    pad_term = n_pad * jnp.exp2(jnp.where(n_pad > 0.0, -m_win, 0.0))
    o_win = acc[:, :HEAD_DIM] * (1.0 / (acc[:, HEAD_DIM:HEAD_DIM + 1] + pad_term))

    gates = jax.nn.sigmoid(gl_ref[0])
    gates = pltpu.roll(gates, (grp * (LANES - nh * N_NSA_BRANCHES)) % LANES, axis=1)
    for r in range(nh):
        sl = slice(r * tq, (r + 1) * tq)
        c = r * N_NSA_BRANCHES
        o_r = (gates[:, c:c + 1] * o_cmp[sl] + gates[:, c + 1:c + 2] * o_slc[sl] + gates[:, c + 2:c + 3] * o_win[sl])
        o_ref[0, :, r * HEAD_DIM:(r + 1) * HEAD_DIM] = o_r.astype(o_ref.dtype)


def _attention(proj3, gate_logits3, k_cmp, v_cmp, overlap_t, block_cols, *, tq=ATTN_TILE):
    b = proj3.shape[0]
    gw = HEADS_PER_GROUP * HEAD_DIM
    kv = lambda off: pl.BlockSpec((1, SEQ, HEAD_DIM), lambda i, g, t: (i, 0, off // HEAD_DIM + g))
    cmp_spec = pl.BlockSpec((1, 1, N_CHUNK, HEAD_DIM), lambda i, g, t: (i, g, 0, 0))
    rows = HEADS_PER_GROUP * tq
    return pl.pallas_call(
        functools.partial(_attn_kernel, tq=tq),
        grid=(b, N_KV_GROUPS, SEQ // tq),
        in_specs=[
            pl.BlockSpec((1, tq, gw), lambda i, g, t: (i, t, OFF_Q // gw + g)),
            kv(OFF_KS), kv(OFF_VS), kv(OFF_KW), kv(OFF_VW),
            cmp_spec, cmp_spec,
            pl.BlockSpec((1, tq, LANES), lambda i, g, t: (i, t, 0)),
            pl.BlockSpec((N_SLC, N_CHUNK), lambda i, g, t: (0, 0)),
            pl.BlockSpec((SEQ, LANES), lambda i, g, t: (0, 0)),
        ],
        out_specs=pl.BlockSpec((1, tq, gw), lambda i, g, t: (i, t, g)),
        out_shape=jax.ShapeDtypeStruct((b, SEQ, D_Q), BF16),
        scratch_shapes=[
            pltpu.VMEM((SEQ // tq, rows, tq), F32),
            pltpu.VMEM((rows, LANES), F32),
            pltpu.VMEM((rows, 2 * HEAD_DIM), F32),
        ],
        compiler_params=_params("parallel", "parallel", "arbitrary"),
        name="nsa_attention",
    )(proj3, proj3, proj3, proj3, proj3, k_cmp, v_cmp, gate_logits3, overlap_t, block_cols)


def _conv_kernel(a_ref, b_ref, ah_ref, bh_ref, w_ref, bias_ref, lg_ref, lb_ref, o_ref, u_ref, y_ref, *, ts):
    nc = D_CONV // LANES
    first = pl.program_id(1) == 0
    u_main = a_ref[0].astype(F32) * jax.nn.sigmoid(b_ref[0].astype(F32))
    u_halo = ah_ref[0].astype(F32) * jax.nn.sigmoid(bh_ref[0].astype(F32))
    u_halo = jnp.where(first, 0.0, u_halo)
    for c in range(nc):
        u_ref[c, 0:HALO, :] = u_halo[:, c * LANES:(c + 1) * LANES]
        u_ref[c, HALO:HALO + ts, :] = u_main[:, c * LANES:(c + 1) * LANES]

    rc = 64
    base = HALO - (CONV_WIDTH - 1)

    def chunk_body(c, carry):
        for r0 in range(0, ts, rc):
            acc = jnp.zeros((rc, LANES), F32)
            for j in range(CONV_WIDTH):
                acc = acc + u_ref[c, pl.ds(base + r0 + j, rc), :] * w_ref[c, j:j + 1, :]
            y_ref[c, r0:r0 + rc, :] = acc
        return carry

    lax.fori_loop(0, nc, chunk_body, 0)

    y = jnp.concatenate([y_ref[c] for c in range(nc)], axis=1) + bias_ref[...]
    mu = jnp.mean(y, axis=-1, keepdims=True)
    d = y - mu
    var = jnp.mean(d * d, axis=-1, keepdims=True)
    z = d * lax.rsqrt(var + LN_EPS) * lg_ref[...] + lb_ref[...]
    o_ref[0] = (z * jax.nn.sigmoid(z)).astype(o_ref.dtype)


def _conformer_conv(proj3, w_chunks, b_dw, ln_g, ln_b, *, ts=256):
    b = proj3.shape[0]
    nc = D_CONV // LANES
    per = ts // HALO
    main = lambda off: pl.BlockSpec((1, ts, D_CONV), lambda i, t: (i, t, off // D_CONV))
    halo = lambda off: pl.BlockSpec((1, HALO, D_CONV), lambda i, t: (i, jnp.maximum(t * per - 1, 0), off // D_CONV))
    vec = pl.BlockSpec((1, D_CONV), lambda i, t: (0, 0))
    return pl.pallas_call(
        functools.partial(_conv_kernel, ts=ts),
        grid=(b, SEQ // ts),
        in_specs=[main(OFF_GLU_A), main(OFF_GLU_B), halo(OFF_GLU_A), halo(OFF_GLU_B),
                  pl.BlockSpec((nc, HALO, LANES), lambda i, t: (0, 0, 0)), vec, vec, vec],
        out_specs=pl.BlockSpec((1, ts, D_CONV), lambda i, t: (i, t, 0)),
        out_shape=jax.ShapeDtypeStruct((b, SEQ, D_CONV), BF16),
        scratch_shapes=[pltpu.VMEM((nc, HALO + ts, LANES), F32), pltpu.VMEM((nc, ts, LANES), F32)],
        compiler_params=_params("parallel", "arbitrary"),
        name="conformer_conv",
    )(proj3, proj3, proj3, proj3, w_chunks, b_dw.reshape(1, D_CONV), ln_g.reshape(1, D_CONV), ln_b.reshape(1, D_CONV))


def _overlap_t():
    cmp_start = np.arange(N_CHUNK) * CMP_STRIDE
    slc_start = np.arange(N_SLC) * SLC_BLOCK
    ov = ((cmp_start[None, :] < slc_start[:, None] + SLC_BLOCK) & (cmp_start[None, :] + CMP_BLOCK > slc_start[:, None])
          & (np.arange(N_CHUNK)[None, :] < N_CMP))
    return jnp.asarray(ov.astype(np.float32), dtype=BF16)


def _block_cols():
    own = (np.arange(SEQ)[:, None] // SLC_BLOCK) == np.arange(LANES)[None, :]
    return jnp.asarray(np.where(own, -MASK_BIG, 0.0).astype(np.float32), dtype=BF16)


def _layer(x, norm_mix_pre, w_in, pos_cmp_k, w_cmp_k1, w_cmp_k2, pos_cmp_v, w_cmp_v1, w_cmp_v2,
           w_dw, b_dw, ln_conv_g, ln_conv_b, w_conv_out, w_attn_out, w_out, norm_mix_post,
           norm_mlp_pre, w_up, w_down, norm_mlp_post):
    b, s, d = x.shape
    m = b * s
    x2 = x.reshape(m, d)

    w_main = jnp.concatenate([w_in[:, :OFF_GATE_A], w_in[:, OFF_GATE_A + N_GATE:]], axis=1).astype(BF16)
    w_gate = jnp.pad(w_in[:, OFF_GATE_A:OFF_GATE_A + N_GATE], ((0, 0), (0, LANES - N_GATE))).astype(BF16)
    col_scale = jnp.ones((D_PROJ,), F32).at[OFF_Q:OFF_Q + D_Q].set(HEAD_DIM ** -0.5 * LOG2_E)
    half = CMP_STRIDE * HEAD_DIM

    u = _rmsnorm(x2, norm_mix_pre)
    proj = _matmul(u, w_main, BF16, bm=1024, bn=1024, name="in_proj", col_scale=col_scale)
    gate_logits = _matmul(u, w_gate, F32, bm=1024, bn=LANES, name="in_proj_gates")
    proj3 = proj.reshape(b, s, D_PROJ)

    def chunks(off):
        t = proj3[:, :, off:off + D_KV].reshape(b, N_CHUNK, CMP_STRIDE, N_KV_GROUPS, HEAD_DIM)
        return t.transpose(0, 3, 1, 2, 4).reshape(b, N_KV_GROUPS, N_CHUNK, half)

    k_cmp, v_cmp = _compress(
        chunks(OFF_KC), chunks(OFF_VC), pos_cmp_k.reshape(2, half), pos_cmp_v.reshape(2, half),
        w_cmp_k1.reshape(2, half, CMP_HIDDEN).astype(BF16), w_cmp_v1.reshape(2, half, CMP_HIDDEN).astype(BF16),
        w_cmp_k2.astype(BF16), w_cmp_v2.astype(BF16))

    attn = _attention(proj3, gate_logits.reshape(b, s, LANES), k_cmp, v_cmp, _overlap_t(), _block_cols())

    w_chunks = jnp.pad(w_dw.reshape(CONV_WIDTH, D_CONV), ((0, HALO - CONV_WIDTH), (0, 0)))
    w_chunks = w_chunks.reshape(HALO, D_CONV // LANES, LANES).transpose(1, 0, 2)
    conv = _conformer_conv(proj3, w_chunks, b_dw, ln_conv_g, ln_conv_b)

    merged = _merge(conv.reshape(m, D_CONV), attn.reshape(m, D_Q), w_conv_out.astype(BF16), w_attn_out.astype(BF16), proj)
    z = _matmul(merged, w_out.astype(BF16), F32, bm=1024, bn=1024, name="out_proj")
    x1, h = _post_mix(x2, z, norm_mix_post, norm_mlp_pre)

    hidden = _matmul(h, w_up.astype(BF16), BF16, bm=1024, bn=1024, name="mlp_up", relu2=True)
    y = _matmul_ktiled(hidden, w_down.astype(BF16), bm=1024, bn=1024, bk=4096, name="mlp_down")
    return _post_mlp(x1, y, norm_mlp_post).reshape(b, s, d)


def kernel(x, norm_mix_pre, w_in, pos_cmp_k, w_cmp_k1, w_cmp_k2, pos_cmp_v, w_cmp_v1, w_cmp_v2, w_dw, b_dw,
           ln_conv_g, ln_conv_b, w_conv_out, w_attn_out, w_out, norm_mix_post, norm_mlp_pre, w_up, w_down,
           norm_mlp_post):
    for l in range(norm_mix_pre.shape[0]):
        x = _layer(x, norm_mix_pre[l], w_in[l], pos_cmp_k[l], w_cmp_k1[l], w_cmp_k2[l], pos_cmp_v[l], w_cmp_v1[l],
                   w_cmp_v2[l], w_dw[l], b_dw[l], ln_conv_g[l], ln_conv_b[l], w_conv_out[l], w_attn_out[l], w_out[l],
                   norm_mix_post[l], norm_mlp_pre[l], w_up[l], w_down[l], norm_mlp_post[l])
    return x
```

```python
import functools

import numpy as np
import jax
import jax.numpy as jnp
from jax import lax
from jax.experimental import pallas as pl
from jax.experimental.pallas import tpu as pltpu

D_MODEL = 4096
SEQ = 2048
D_CONV = D_MODEL // 2
CONV_WIDTH = 31
HEAD_DIM = 128
N_HEADS = 16
N_KV_GROUPS = 4
HEADS_PER_GROUP = N_HEADS // N_KV_GROUPS
CMP_BLOCK = 32
CMP_STRIDE = 16
CMP_HIDDEN = 2 * HEAD_DIM
SLC_BLOCK = 64
N_SELECT = 16
WINDOW = 512
N_NSA_BRANCHES = 3
D_FF = 4 * D_MODEL
D_Q = N_HEADS * HEAD_DIM
D_KV = N_KV_GROUPS * HEAD_DIM
N_GATE = N_NSA_BRANCHES * N_HEADS
D_MAIN = 2 * D_CONV + D_Q + 6 * D_KV
N_CHUNK = SEQ // CMP_STRIDE
N_CMP = N_CHUNK - CMP_BLOCK // CMP_STRIDE + 1
N_SLC = SEQ // SLC_BLOCK

NORM_EPS = 1e-6
LN_EPS = 1e-5
FORCED_SCORE = 1e4
MASKED_SCORE = -1e4
NEG_INF = -1e30
MASK_BIG = 2.0 ** 100
LOG2_E = float(np.log2(np.e))

LANES = 128
HALO = 32
ATTN_TILE = 256
MM_BM = 1024
MM_BN = 512
VMEM_LIMIT = 56 * 1024 * 1024

OFF_GLU_A = 0
OFF_GLU_B = D_CONV
OFF_Q = 2 * D_CONV
OFF_KC = OFF_Q + D_Q
OFF_VC = OFF_KC + D_KV
OFF_KS = OFF_VC + D_KV
OFF_VS = OFF_KS + D_KV
OFF_KW = OFF_VS + D_KV
OFF_VW = OFF_KW + D_KV

F32 = jnp.float32
BF16 = jnp.bfloat16


def _params(*sem):
    return pltpu.CompilerParams(dimension_semantics=sem, vmem_limit_bytes=VMEM_LIMIT)


def _rmsnorm_kernel(x_ref, g_ref, o_ref):
    x = x_ref[...]
    y = x * lax.rsqrt(jnp.mean(x * x, axis=-1, keepdims=True) + NORM_EPS)
    o_ref[...] = (y * g_ref[...]).astype(o_ref.dtype)


def _rmsnorm(x, g, bm=256):
    m, d = x.shape
    return pl.pallas_call(
        _rmsnorm_kernel,
        grid=(m // bm,),
        in_specs=[pl.BlockSpec((bm, d), lambda i: (i, 0)), pl.BlockSpec((1, d), lambda i: (0, 0))],
        out_specs=pl.BlockSpec((bm, d), lambda i: (i, 0)),
        out_shape=jax.ShapeDtypeStruct((m, d), BF16),
        compiler_params=_params("parallel"),
        name="rmsnorm_in",
    )(x, g.reshape(1, d))


def _post_mix_kernel(x_ref, z_ref, g1_ref, g2_ref, x1_ref, h_ref):
    z = z_ref[...].astype(F32)
    zn = z * lax.rsqrt(jnp.mean(z * z, axis=-1, keepdims=True) + NORM_EPS)
    x1 = x_ref[...] + zn * g1_ref[...]
    x1_ref[...] = x1
    hn = x1 * lax.rsqrt(jnp.mean(x1 * x1, axis=-1, keepdims=True) + NORM_EPS)
    h_ref[...] = (hn * g2_ref[...]).astype(h_ref.dtype)


def _post_mix(x, z, g1, g2, bm=256):
    m, d = x.shape
    row = pl.BlockSpec((bm, d), lambda i: (i, 0))
    vec = pl.BlockSpec((1, d), lambda i: (0, 0))
    return pl.pallas_call(
        _post_mix_kernel,
        grid=(m // bm,),
        in_specs=[row, row, vec, vec],
        out_specs=[row, row],
        out_shape=[jax.ShapeDtypeStruct((m, d), F32), jax.ShapeDtypeStruct((m, d), BF16)],
        compiler_params=_params("parallel"),
        name="post_mix_norm",
    )(x, z, g1.reshape(1, d), g2.reshape(1, d))


def _post_mlp_kernel(x_ref, y_ref, g_ref, o_ref):
    y = y_ref[...].astype(F32)
    yn = y * lax.rsqrt(jnp.mean(y * y, axis=-1, keepdims=True) + NORM_EPS)
    o_ref[...] = x_ref[...] + yn * g_ref[...]


def _post_mlp(x, y, g, bm=256):
    m, d = x.shape
    row = pl.BlockSpec((bm, d), lambda i: (i, 0))
    return pl.pallas_call(
        _post_mlp_kernel,
        grid=(m // bm,),
        in_specs=[row, row, pl.BlockSpec((1, d), lambda i: (0, 0))],
        out_specs=row,
        out_shape=jax.ShapeDtypeStruct((m, d), F32),
        compiler_params=_params("parallel"),
        name="post_mlp_norm",
    )(x, y, g.reshape(1, d))


def _mm_scale_kernel(a_ref, w_ref, s_ref, o_ref):
    acc = jnp.dot(a_ref[...], w_ref[...].astype(BF16), preferred_element_type=F32)
    o_ref[...] = (acc * s_ref[...]).astype(o_ref.dtype)


def _mm_plain_kernel(a_ref, w_ref, o_ref):
    o_ref[...] = jnp.dot(a_ref[...], w_ref[...].astype(BF16), preferred_element_type=F32).astype(o_ref.dtype)


def _mm_relu2_kernel(a_ref, w_ref, o_ref):
    acc = jnp.dot(a_ref[...], w_ref[...].astype(BF16), preferred_element_type=F32)
    r = jnp.maximum(acc, 0.0)
    o_ref[...] = (r * r).astype(o_ref.dtype)


def _matmul(a, w, n, out_dtype, *, name, bm=MM_BM, bn=MM_BN, col_scale=None, relu2=False):
    m, k = a.shape
    in_specs = [pl.BlockSpec((bm, k), lambda i, j: (i, 0)), pl.BlockSpec((k, bn), lambda i, j: (0, j))]
    args = [a, w]
    if col_scale is not None:
        body = _mm_scale_kernel
        in_specs.append(pl.BlockSpec((1, bn), lambda i, j: (0, j)))
        args.append(col_scale.reshape(1, n))
    else:
        body = _mm_relu2_kernel if relu2 else _mm_plain_kernel
    return pl.pallas_call(
        body,
        grid=(m // bm, n // bn),
        in_specs=in_specs,
        out_specs=pl.BlockSpec((bm, bn), lambda i, j: (i, j)),
        out_shape=jax.ShapeDtypeStruct((m, n), out_dtype),
        compiler_params=_params("parallel", "arbitrary"),
        name=name,
    )(*args)


def _mm_kacc_kernel(a_ref, w_ref, o_ref, acc_ref):
    kk = pl.program_id(2)
    part = jnp.dot(a_ref[...], w_ref[...].astype(BF16), preferred_element_type=F32)

    @pl.when(kk == 0)
    def _():
        acc_ref[...] = part

    @pl.when(kk > 0)
    def _():
        acc_ref[...] += part

    @pl.when(kk == pl.num_programs(2) - 1)
    def _():
        o_ref[...] = acc_ref[...].astype(o_ref.dtype)


def _matmul_ktiled(a, w, out_dtype, *, bk, name, bm=MM_BM, bn=MM_BN):
    m, k = a.shape
    n = w.shape[1]
    return pl.pallas_call(
        _mm_kacc_kernel,
        grid=(m // bm, n // bn, k // bk),
        in_specs=[pl.BlockSpec((bm, bk), lambda i, j, kk: (i, kk)), pl.BlockSpec((bk, bn), lambda i, j, kk: (kk, j))],
        out_specs=pl.BlockSpec((bm, bn), lambda i, j, kk: (i, j)),
        out_shape=jax.ShapeDtypeStruct((m, n), out_dtype),
        scratch_shapes=[pltpu.VMEM((bm, bn), F32)],
        compiler_params=_params("parallel", "parallel", "arbitrary"),
        name=name,
    )(a, w)


def _merge_kernel(c_ref, a_ref, wc_ref, wa_ref, ga_ref, gb_ref, o_ref):
    yc = jnp.dot(c_ref[...], wc_ref[...].astype(BF16), preferred_element_type=F32)
    ya = jnp.dot(a_ref[...], wa_ref[...].astype(BF16), preferred_element_type=F32)
    ga = jax.nn.sigmoid(ga_ref[...].astype(F32))
    gb = jax.nn.sigmoid(gb_ref[...].astype(F32))
    o_ref[...] = (ga * yc + gb * ya).astype(o_ref.dtype)


def _merge(conv_act, attn_act, wc, wa, gates_ab, *, bm=MM_BM, bn=MM_BN):
    m, kc = conv_act.shape
    ka = attn_act.shape[1]
    n = wc.shape[1]
    jb = n // bn
    return pl.pallas_call(
        _merge_kernel,
        grid=(m // bm, n // bn),
        in_specs=[
            pl.BlockSpec((bm, kc), lambda i, j: (i, 0)),
            pl.BlockSpec((bm, ka), lambda i, j: (i, 0)),
            pl.BlockSpec((kc, bn), lambda i, j: (0, j)),
            pl.BlockSpec((ka, bn), lambda i, j: (0, j)),
            pl.BlockSpec((bm, bn), lambda i, j: (i, j)),
            pl.BlockSpec((bm, bn), lambda i, j: (i, jb + j)),
        ],
        out_specs=pl.BlockSpec((bm, bn), lambda i, j: (i, j)),
        out_shape=jax.ShapeDtypeStruct((m, n), BF16),
        compiler_params=_params("parallel", "arbitrary"),
        name="gated_merge",
    )(conv_act, attn_act, wc, wa, gates_ab, gates_ab)


def _gelu_tanh(x):
    return 0.5 * x * (1.0 + jnp.tanh(np.sqrt(2.0 / np.pi) * (x + 0.044715 * (x * x * x))))


def _compress_kernel(xk_ref, xv_ref, pk_ref, pv_ref, w1k_ref, w1v_ref, w2k_ref, w2v_ref, ok_ref, ov_ref):
    for x_ref, p_ref, w1_ref, w2_ref, o_ref in ((xk_ref, pk_ref, w1k_ref, w2k_ref, ok_ref),
                                                (xv_ref, pv_ref, w1v_ref, w2v_ref, ov_ref)):
        x = x_ref[0, 0].astype(F32)
        first = jnp.dot((x + p_ref[0:1, :]).astype(BF16), w1_ref[0].astype(BF16), preferred_element_type=F32)
        second = jnp.dot((x + p_ref[1:2, :]).astype(BF16), w1_ref[1].astype(BF16), preferred_element_type=F32)
        h = first + pltpu.roll(second, N_CHUNK - 1, axis=0)
        o_ref[0, 0] = jnp.dot(_gelu_tanh(h).astype(BF16), w2_ref[...].astype(BF16),
                              preferred_element_type=F32).astype(o_ref.dtype)


def _compress(xk, xv, pk, pv, w1k, w1v, w2k, w2v):
    b = xk.shape[0]
    half = CMP_STRIDE * HEAD_DIM
    xspec = pl.BlockSpec((1, 1, N_CHUNK, half), lambda i, g: (i, g, 0, 0))
    pspec = pl.BlockSpec((2, half), lambda i, g: (0, 0))
    w1spec = pl.BlockSpec((2, half, CMP_HIDDEN), lambda i, g: (0, 0, 0))
    w2spec = pl.BlockSpec((CMP_HIDDEN, HEAD_DIM), lambda i, g: (0, 0))
    ospec = pl.BlockSpec((1, 1, N_CHUNK, HEAD_DIM), lambda i, g: (i, g, 0, 0))
    oshape = jax.ShapeDtypeStruct((b, N_KV_GROUPS, N_CHUNK, HEAD_DIM), BF16)
    return pl.pallas_call(
        _compress_kernel,
        grid=(b, N_KV_GROUPS),
        in_specs=[xspec, xspec, pspec, pspec, w1spec, w1spec, w2spec, w2spec],
        out_specs=[ospec, ospec],
        out_shape=[oshape, oshape],
        compiler_params=_params("parallel", "parallel"),
        name="compress_kv",
    )(xk, xv, pk, pv, w1k, w1v, w2k, w2v)


def _attn_kernel(q_ref, ks_ref, vs_ref, kw_ref, vw_ref, kc_ref, vc_ref, gl_ref, ovl_ref, blk_ref, o_ref,
                 s_ref, mx_ref, acc_ref, *, tq):
    tk = tq
    grp = pl.program_id(1)
    qi = pl.program_id(2)
    q0 = qi * tq
    nh = HEADS_PER_GROUP
    rows = nh * tq
    nt = (((1,), (1,)), ((), ()))

    q = q_ref[0]
    qs = jnp.concatenate([q[:, r * HEAD_DIM:(r + 1) * HEAD_DIM] for r in range(nh)], axis=0)

    s = lax.dot_general(qs, kc_ref[0, 0], nt, preferred_element_type=F32).reshape(nh, tq, N_CHUNK)
    n_idx = lax.broadcasted_iota(jnp.int32, (tq, N_CHUNK), 1)
    t_idx = q0 + lax.broadcasted_iota(jnp.int32, (tq, N_CHUNK), 0)
    vis = ((n_idx * CMP_STRIDE + (CMP_BLOCK - 1) <= t_idx) & (n_idx < N_CMP))[None]
    sm = jnp.where(vis, s, NEG_INF)
    mx = jnp.maximum(jnp.max(sm, axis=-1, keepdims=True), 0.5 * NEG_INF)
    e = jnp.exp2(sm - mx)
    p = e * (1.0 / jnp.maximum(jnp.sum(e, axis=-1, keepdims=True), 1e-30))
    o_cmp = jnp.dot(p.reshape(rows, N_CHUNK).astype(BF16), vc_ref[0, 0], preferred_element_type=F32)

    p_sum = p[0] + p[1] + p[2] + p[3]
    p_hi = p_sum.astype(BF16)
    rem = p_sum - p_hi.astype(F32)
    p_mid = rem.astype(BF16)
    p_lo = (rem - p_mid.astype(F32)).astype(BF16)
    ovl = ovl_ref[...]
    imp = (lax.dot_general(ovl, p_hi, nt, preferred_element_type=F32)
           + lax.dot_general(ovl, p_mid, nt, preferred_element_type=F32)
           + lax.dot_general(ovl, p_lo, nt, preferred_element_type=F32))

    j_idx = lax.broadcasted_iota(jnp.int32, (N_SLC, tq), 0)
    cur = jnp.right_shift(q0 + lax.broadcasted_iota(jnp.int32, (N_SLC, tq), 1), int(np.log2(SLC_BLOCK)))
    valid = j_idx <= cur
    forced = (j_idx == 0) | (j_idx == cur) | (j_idx == cur - 1)
    score = jnp.where(forced & valid, FORCED_SCORE, jnp.where(valid, imp, MASKED_SCORE))
    rank = jnp.zeros((N_SLC, tq), jnp.int32)
    for k in range(N_SLC):
        sk = score[k:k + 1, :]
        beats = (sk > score) | ((sk == score) & (j_idx > k))
        rank = rank + beats.astype(jnp.int32)
    unsel_t = jnp.where(rank < N_SELECT, 0.0, 1.0)
    unsel_t = jnp.concatenate([unsel_t, jnp.zeros((LANES - N_SLC, tq), F32)], axis=0)
    unsel = jnp.transpose(unsel_t).astype(BF16)

    q_slc = jnp.concatenate([qs, jnp.concatenate([unsel] * nh, axis=0)], axis=1)

    local_q = lax.broadcasted_iota(jnp.int32, (tq, tk), 0)
    local_k = lax.broadcasted_iota(jnp.int32, (tq, tk), 1)
    causal_bias = jnp.where(local_k <= local_q, 0.0, NEG_INF)
    far_bias = jnp.where(local_k > local_q, 0.0, NEG_INF)

    def pass1(qa, k_ref, with_blocks, kt0, slot0, biases, init):
        n = len(biases)
        k0 = pl.multiple_of(kt0 * tk, tk)
        k = k_ref[0, pl.ds(k0, n * tk), :]
        if with_blocks:
            k = jnp.concatenate([k, blk_ref[pl.ds(k0, n * tk), :]], axis=1)
        sc = lax.dot_general(qa, k, nt, preferred_element_type=F32)
        m = None
        for i, bias in enumerate(biases):
            part = sc[:, i * tk:(i + 1) * tk]
            if bias is not None:
                part = (part.reshape(nh, tq, tk) + bias[None]).reshape(rows, tk)
            s_ref[slot0 + i] = part
            for c in range(tk // LANES):
                piece = part[:, c * LANES:(c + 1) * LANES]
                m = piece if m is None else jnp.maximum(m, piece)
        mx_ref[...] = m if init else jnp.maximum(mx_ref[...], m)

    def pass2(v_ref, kt0, slot0, n, init):
        k0 = pl.multiple_of(kt0 * tk, tk)
        v_ext = jnp.concatenate([v_ref[0, pl.ds(k0, n * tk), :], jnp.ones((n * tk, LANES), BF16)], axis=1)
        m_rep = mx_ref[...]
        m_wide = jnp.concatenate([m_rep] * (tk // LANES), axis=1)
        pr = jnp.concatenate([jnp.exp2(s_ref[slot0 + i] - m_wide).astype(BF16) for i in range(n)], axis=1)
        d = jnp.dot(pr, v_ext, preferred_element_type=F32)
        if init:
            acc_ref[...] = d
        else:
            acc_ref[...] += d

    def spread_row_max():
        mx_ref[...] = jnp.broadcast_to(jnp.max(mx_ref[...], axis=-1, keepdims=True), (rows, LANES))

    pairs = jnp.right_shift(qi, 1)
    odd = jnp.bitwise_and(qi, 1) == 1

    pass1(q_slc, ks_ref, True, qi, qi, [causal_bias], True)

    def slc_pass1(i, carry):
        pass1(q_slc, ks_ref, True, 2 * i, 2 * i, [None, None], False)
        return carry

    lax.fori_loop(0, pairs, slc_pass1, 0)

    @pl.when(odd)
    def _():
        pass1(q_slc, ks_ref, True, qi - 1, qi - 1, [None], False)

    spread_row_max()
    pass2(vs_ref, qi, qi, 1, True)

    def slc_pass2(i, carry):
        pass2(vs_ref, 2 * i, 2 * i, 2, False)
        return carry

    lax.fori_loop(0, pairs, slc_pass2, 0)

    @pl.when(odd)
    def _():
        pass2(vs_ref, qi - 1, qi - 1, 1, False)

    acc = acc_ref[...]
    o_slc = acc[:, :HEAD_DIM] * (1.0 / acc[:, HEAD_DIM:])

    @pl.when(qi >= 2)
    def _():
        pass1(qs, kw_ref, False, qi - 2, 0, [far_bias, None, causal_bias], True)

    @pl.when(qi == 1)
    def _():
        pass1(qs, kw_ref, False, 0, 0, [None, causal_bias], True)

    @pl.when(qi == 0)
    def _():
        pass1(qs, kw_ref, False, 0, 0, [causal_bias], True)

    n_pad = jnp.maximum(WINDOW - 1 - (q0 + lax.broadcasted_iota(jnp.int32, (tq, LANES), 0)), 0).astype(F32)
    n_pad = jnp.concatenate([n_pad] * nh, axis=0)
    m_win = jnp.broadcast_to(jnp.max(mx_ref[...], axis=-1, keepdims=True), (rows, LANES))
    m_win = jnp.where(n_pad > 0.0, jnp.maximum(m_win, 0.0), m_win)
    mx_ref[...] = m_win

    @pl.when(qi >= 2)
    def _():
        pass2(vw_ref, qi - 2, 0, 3, True)

    @pl.when(qi == 1)
    def _():
        pass2(vw_ref, 0, 0, 2, True)

    @pl.when(qi == 0)
    def _():
        pass2(vw_ref, 0, 0, 1, True)

    acc = acc_ref[...]
    pad_term = n_pad * jnp.exp2(jnp.where(n_pad > 0.0, -m_win, 0.0))
    o_win = acc[:, :HEAD_DIM] * (1.0 / (acc[:, HEAD_DIM:] + pad_term))

    gates = jax.nn.sigmoid(gl_ref[0])
    gates = pltpu.roll(gates, (grp * (LANES - nh * N_NSA_BRANCHES)) % LANES, axis=1)
    for r in range(nh):
        sl = slice(r * tq, (r + 1) * tq)
        c = r * N_NSA_BRANCHES
        o_r = (gates[:, c:c + 1] * o_cmp[sl] + gates[:, c + 1:c + 2] * o_slc[sl] + gates[:, c + 2:c + 3] * o_win[sl])
        o_ref[0, :, r * HEAD_DIM:(r + 1) * HEAD_DIM] = o_r.astype(o_ref.dtype)


def _attention(proj3, gate_logits3, k_cmp, v_cmp, overlap_t, block_cols, *, tq=ATTN_TILE):
    b = proj3.shape[0]
    gw = HEADS_PER_GROUP * HEAD_DIM
    kv = lambda off: pl.BlockSpec((1, SEQ, HEAD_DIM), lambda i, g, t: (i, 0, off // HEAD_DIM + g))
    cmp_spec = pl.BlockSpec((1, 1, N_CHUNK, HEAD_DIM), lambda i, g, t: (i, g, 0, 0))
    rows = HEADS_PER_GROUP * tq
    return pl.pallas_call(
        functools.partial(_attn_kernel, tq=tq),
        grid=(b, N_KV_GROUPS, SEQ // tq),
        in_specs=[
            pl.BlockSpec((1, tq, gw), lambda i, g, t: (i, t, OFF_Q // gw + g)),
            kv(OFF_KS), kv(OFF_VS), kv(OFF_KW), kv(OFF_VW),
            cmp_spec, cmp_spec,
            pl.BlockSpec((1, tq, LANES), lambda i, g, t: (i, t, 0)),
            pl.BlockSpec((N_SLC, N_CHUNK), lambda i, g, t: (0, 0)),
            pl.BlockSpec((SEQ, LANES), lambda i, g, t: (0, 0)),
        ],
        out_specs=pl.BlockSpec((1, tq, gw), lambda i, g, t: (i, t, g)),
        out_shape=jax.ShapeDtypeStruct((b, SEQ, D_Q), BF16),
        scratch_shapes=[
            pltpu.VMEM((SEQ // tq, rows, tq), F32),
            pltpu.VMEM((rows, LANES), F32),
            pltpu.VMEM((rows, 2 * HEAD_DIM), F32),
        ],
        compiler_params=_params("parallel", "parallel", "arbitrary"),
        name="nsa_attention",
    )(proj3, proj3, proj3, proj3, proj3, k_cmp, v_cmp, gate_logits3, overlap_t, block_cols)


def _conv_kernel(a_ref, b_ref, ah_ref, bh_ref, w_ref, bias_ref, lg_ref, lb_ref, o_ref, u_ref, y_ref, *, ts):
    nc = D_CONV // LANES
    first = pl.program_id(1) == 0
    u_main = a_ref[0].astype(F32) * jax.nn.sigmoid(b_ref[0].astype(F32))
    u_halo = ah_ref[0].astype(F32) * jax.nn.sigmoid(bh_ref[0].astype(F32))
    u_halo = jnp.where(first, 0.0, u_halo)
    for c in range(nc):
        u_ref[c, 0:HALO, :] = u_halo[:, c * LANES:(c + 1) * LANES]
        u_ref[c, HALO:HALO + ts, :] = u_main[:, c * LANES:(c + 1) * LANES]

    rc = 64
    base = HALO - (CONV_WIDTH - 1)

    def chunk_body(c, carry):
        for r0 in range(0, ts, rc):
            acc = jnp.zeros((rc, LANES), F32)
            for j in range(CONV_WIDTH):
                acc = acc + u_ref[c, pl.ds(base + r0 + j, rc), :] * w_ref[c, j:j + 1, :]
            y_ref[c, r0:r0 + rc, :] = acc
        return carry

    lax.fori_loop(0, nc, chunk_body, 0)

    y = jnp.concatenate([y_ref[c] for c in range(nc)], axis=1) + bias_ref[...]
    mu = jnp.mean(y, axis=-1, keepdims=True)
    d = y - mu
    var = jnp.mean(d * d, axis=-1, keepdims=True)
    z = d * lax.rsqrt(var + LN_EPS) * lg_ref[...] + lb_ref[...]
    o_ref[0] = (z * jax.nn.sigmoid(z)).astype(o_ref.dtype)


def _conformer_conv(proj3, w_chunks, b_dw, ln_g, ln_b, *, ts=256):
    b = proj3.shape[0]
    nc = D_CONV // LANES
    per = ts // HALO
    main = lambda off: pl.BlockSpec((1, ts, D_CONV), lambda i, t: (i, t, off // D_CONV))
    halo = lambda off: pl.BlockSpec((1, HALO, D_CONV), lambda i, t: (i, jnp.maximum(t * per - 1, 0), off // D_CONV))
    vec = pl.BlockSpec((1, D_CONV), lambda i, t: (0, 0))
    return pl.pallas_call(
        functools.partial(_conv_kernel, ts=ts),
        grid=(b, SEQ // ts),
        in_specs=[main(OFF_GLU_A), main(OFF_GLU_B), halo(OFF_GLU_A), halo(OFF_GLU_B),
                  pl.BlockSpec((nc, HALO, LANES), lambda i, t: (0, 0, 0)), vec, vec, vec],
        out_specs=pl.BlockSpec((1, ts, D_CONV), lambda i, t: (i, t, 0)),
        out_shape=jax.ShapeDtypeStruct((b, SEQ, D_CONV), BF16),
        scratch_shapes=[pltpu.VMEM((nc, HALO + ts, LANES), F32), pltpu.VMEM((nc, ts, LANES), F32)],
        compiler_params=_params("parallel", "arbitrary"),
        name="conformer_conv",
    )(proj3, proj3, proj3, proj3, w_chunks, b_dw.reshape(1, D_CONV), ln_g.reshape(1, D_CONV), ln_b.reshape(1, D_CONV))


def _overlap_t():
    cmp_start = np.arange(N_CHUNK) * CMP_STRIDE
    slc_start = np.arange(N_SLC) * SLC_BLOCK
    ov = ((cmp_start[None, :] < slc_start[:, None] + SLC_BLOCK) & (cmp_start[None, :] + CMP_BLOCK > slc_start[:, None])
          & (np.arange(N_CHUNK)[None, :] < N_CMP))
    return jnp.asarray(ov.astype(np.float32), dtype=BF16)


def _block_cols():
    own = (np.arange(SEQ)[:, None] // SLC_BLOCK) == np.arange(LANES)[None, :]
    return jnp.asarray(np.where(own, -MASK_BIG, 0.0).astype(np.float32), dtype=BF16)


def _layer(x, norm_mix_pre, w_in, pos_cmp_k, w_cmp_k1, w_cmp_k2, pos_cmp_v, w_cmp_v1, w_cmp_v2,
           w_dw, b_dw, ln_conv_g, ln_conv_b, w_conv_out, w_attn_out, w_out, norm_mix_post,
           norm_mlp_pre, w_up, w_down, norm_mlp_post):
    b, s, d = x.shape
    m = b * s
    x2 = x.reshape(m, d)
    half = CMP_STRIDE * HEAD_DIM

    w_gate = jnp.pad(w_in[:, D_MAIN:D_MAIN + N_GATE], ((0, 0), (0, LANES - N_GATE))).astype(BF16)
    w_gates_ab = w_in[:, D_MAIN + N_GATE:].astype(BF16)
    col_scale = jnp.ones((D_MAIN,), F32).at[OFF_Q:OFF_Q + D_Q].set(HEAD_DIM ** -0.5 * LOG2_E)

    u = _rmsnorm(x2, norm_mix_pre)
    proj = _matmul(u, w_in, D_MAIN, BF16, name="in_proj", col_scale=col_scale)
    gates_ab = _matmul(u, w_gates_ab, 2 * D_MODEL, BF16, name="in_proj_merge_gates")
    gate_logits = _matmul(u, w_gate, LANES, F32, bn=LANES, name="in_proj_branch_gates")
    proj3 = proj.reshape(b, s, D_MAIN)

    def chunks(off):
        t = proj3[:, :, off:off + D_KV].reshape(b, N_CHUNK, CMP_STRIDE, N_KV_GROUPS, HEAD_DIM)
        return t.transpose(0, 3, 1, 2, 4).reshape(b, N_KV_GROUPS, N_CHUNK, half)

    k_cmp, v_cmp = _compress(
        chunks(OFF_KC), chunks(OFF_VC), pos_cmp_k.reshape(2, half), pos_cmp_v.reshape(2, half),
        w_cmp_k1.reshape(2, half, CMP_HIDDEN), w_cmp_v1.reshape(2, half, CMP_HIDDEN), w_cmp_k2, w_cmp_v2)

    attn = _attention(proj3, gate_logits.reshape(b, s, LANES), k_cmp, v_cmp, _overlap_t(), _block_cols())

    w_chunks = jnp.pad(w_dw.reshape(CONV_WIDTH, D_CONV), ((0, HALO - CONV_WIDTH), (0, 0)))
    w_chunks = w_chunks.reshape(HALO, D_CONV // LANES, LANES).transpose(1, 0, 2)
    conv = _conformer_conv(proj3, w_chunks, b_dw, ln_conv_g, ln_conv_b)

    merged = _merge(conv.reshape(m, D_CONV), attn.reshape(m, D_Q), w_conv_out, w_attn_out, gates_ab)
    z = _matmul(merged, w_out, D_MODEL, BF16, name="out_proj")
    x1, h = _post_mix(x2, z, norm_mix_post, norm_mlp_pre)

    hidden = _matmul(h, w_up, D_FF, BF16, name="mlp_up", relu2=True)
    y = _matmul_ktiled(hidden, w_down, BF16, bk=4096, name="mlp_down")
    return _post_mlp(x1, y, norm_mlp_post).reshape(b, s, d)


def kernel(x, norm_mix_pre, w_in, pos_cmp_k, w_cmp_k1, w_cmp_k2, pos_cmp_v, w_cmp_v1, w_cmp_v2, w_dw, b_dw,
           ln_conv_g, ln_conv_b, w_conv_out, w_attn_out, w_out, norm_mix_post, norm_mlp_pre, w_up, w_down,
           norm_mlp_post):
    for l in range(norm_mix_pre.shape[0]):
        x = _layer(x, norm_mix_pre[l], w_in[l], pos_cmp_k[l], w_cmp_k1[l], w_cmp_k2[l], pos_cmp_v[l], w_cmp_v1[l],
                   w_cmp_v2[l], w_dw[l], b_dw[l], ln_conv_g[l], ln_conv_b[l], w_conv_out[l], w_attn_out[l], w_out[l],
                   norm_mix_post[l], norm_mlp_pre[l], w_up[l], w_down[l], norm_mlp_post[l])
    return x
```

```python
import functools

import numpy as np
import jax
import jax.numpy as jnp
from jax import lax
from jax.experimental import pallas as pl
from jax.experimental.pallas import tpu as pltpu

D_MODEL = 4096
SEQ = 2048
D_CONV = D_MODEL // 2
CONV_WIDTH = 31
HEAD_DIM = 128
N_HEADS = 16
N_KV_GROUPS = 4
HEADS_PER_GROUP = N_HEADS // N_KV_GROUPS
CMP_BLOCK = 32
CMP_STRIDE = 16
CMP_HIDDEN = 2 * HEAD_DIM
SLC_BLOCK = 64
N_SELECT = 16
WINDOW = 512
N_NSA_BRANCHES = 3
D_FF = 4 * D_MODEL
D_Q = N_HEADS * HEAD_DIM
D_KV = N_KV_GROUPS * HEAD_DIM
N_GATE = N_NSA_BRANCHES * N_HEADS
D_MAIN = 2 * D_CONV + D_Q + 6 * D_KV
N_CHUNK = SEQ // CMP_STRIDE
N_CMP = N_CHUNK - CMP_BLOCK // CMP_STRIDE + 1
N_SLC = SEQ // SLC_BLOCK

NORM_EPS = 1e-6
LN_EPS = 1e-5
FORCED_SCORE = 1e4
MASKED_SCORE = -1e4
NEG_INF = -1e30
MASK_BIG = 2.0 ** 100
LOG2_E = float(np.log2(np.e))

LANES = 128
SUBLANES = 8
HALO = 32
ATTN_TILE = 256
MM_BM = 1024
MM_BN = 512
VMEM_LIMIT = 56 * 1024 * 1024

OFF_GLU_A = 0
OFF_GLU_B = D_CONV
OFF_Q = 2 * D_CONV
OFF_KC = OFF_Q + D_Q
OFF_VC = OFF_KC + D_KV
OFF_KS = OFF_VC + D_KV
OFF_VS = OFF_KS + D_KV
OFF_KW = OFF_VS + D_KV
OFF_VW = OFF_KW + D_KV

F32 = jnp.float32
BF16 = jnp.bfloat16


def _params(*sem):
    return pltpu.CompilerParams(dimension_semantics=sem, vmem_limit_bytes=VMEM_LIMIT)


def _rmsnorm_kernel(x_ref, g_ref, o_ref):
    x = x_ref[...]
    y = x * lax.rsqrt(jnp.mean(x * x, axis=-1, keepdims=True) + NORM_EPS)
    o_ref[...] = (y * g_ref[...]).astype(o_ref.dtype)


def _rmsnorm(x, g, bm=256):
    m, d = x.shape
    return pl.pallas_call(
        _rmsnorm_kernel,
        grid=(m // bm,),
        in_specs=[pl.BlockSpec((bm, d), lambda i: (i, 0)), pl.BlockSpec((1, d), lambda i: (0, 0))],
        out_specs=pl.BlockSpec((bm, d), lambda i: (i, 0)),
        out_shape=jax.ShapeDtypeStruct((m, d), BF16),
        compiler_params=_params("parallel"),
        name="rmsnorm_in",
    )(x, g.reshape(1, d))


def _post_mix_kernel(x_ref, z_ref, g1_ref, g2_ref, x1_ref, h_ref):
    z = z_ref[...].astype(F32)
    zn = z * lax.rsqrt(jnp.mean(z * z, axis=-1, keepdims=True) + NORM_EPS)
    x1 = x_ref[...] + zn * g1_ref[...]
    x1_ref[...] = x1
    hn = x1 * lax.rsqrt(jnp.mean(x1 * x1, axis=-1, keepdims=True) + NORM_EPS)
    h_ref[...] = (hn * g2_ref[...]).astype(h_ref.dtype)


def _post_mix(x, z, g1, g2, bm=256):
    m, d = x.shape
    row = pl.BlockSpec((bm, d), lambda i: (i, 0))
    vec = pl.BlockSpec((1, d), lambda i: (0, 0))
    return pl.pallas_call(
        _post_mix_kernel,
        grid=(m // bm,),
        in_specs=[row, row, vec, vec],
        out_specs=[row, row],
        out_shape=[jax.ShapeDtypeStruct((m, d), F32), jax.ShapeDtypeStruct((m, d), BF16)],
        compiler_params=_params("parallel"),
        name="post_mix_norm",
    )(x, z, g1.reshape(1, d), g2.reshape(1, d))


def _post_mlp_kernel(x_ref, y_ref, g_ref, o_ref):
    y = y_ref[...].astype(F32)
    yn = y * lax.rsqrt(jnp.mean(y * y, axis=-1, keepdims=True) + NORM_EPS)
    o_ref[...] = x_ref[...] + yn * g_ref[...]


def _post_mlp(x, y, g, bm=256):
    m, d = x.shape
    row = pl.BlockSpec((bm, d), lambda i: (i, 0))
    return pl.pallas_call(
        _post_mlp_kernel,
        grid=(m // bm,),
        in_specs=[row, row, pl.BlockSpec((1, d), lambda i: (0, 0))],
        out_specs=row,
        out_shape=jax.ShapeDtypeStruct((m, d), F32),
        compiler_params=_params("parallel"),
        name="post_mlp_norm",
    )(x, y, g.reshape(1, d))


_NT_DIMS = (((1,), (1,)), ((), ()))


def _mm_nt_scale_kernel(a_ref, wt_ref, s_ref, o_ref):
    acc = lax.dot_general(a_ref[...], wt_ref[...].astype(BF16), _NT_DIMS, preferred_element_type=F32)
    o_ref[...] = (acc * s_ref[...]).astype(o_ref.dtype)


def _mm_nt_kernel(a_ref, wt_ref, o_ref):
    acc = lax.dot_general(a_ref[...], wt_ref[...].astype(BF16), _NT_DIMS, preferred_element_type=F32)
    o_ref[...] = acc.astype(o_ref.dtype)


def _matmul_nt(a, wt, row0, n, out_dtype, *, name, bm=MM_BM, bn=MM_BN, col_scale=None):
    m, k = a.shape
    in_specs = [pl.BlockSpec((bm, k), lambda i, j: (i, 0)),
                pl.BlockSpec((pl.Element(bn), pl.Element(k)),
                             lambda i, j: ((row0 // SUBLANES + j * (bn // SUBLANES)) * SUBLANES, 0))]
    args = [a, wt]
    body = _mm_nt_kernel
    if col_scale is not None:
        body = _mm_nt_scale_kernel
        in_specs.append(pl.BlockSpec((1, bn), lambda i, j: (0, j)))
        args.append(col_scale.reshape(1, n))
    return pl.pallas_call(
        body,
        grid=(m // bm, n // bn),
        in_specs=in_specs,
        out_specs=pl.BlockSpec((bm, bn), lambda i, j: (i, j)),
        out_shape=jax.ShapeDtypeStruct((m, n), out_dtype),
        compiler_params=_params("parallel", "arbitrary"),
        name=name,
    )(*args)


def _mm_plain_kernel(a_ref, w_ref, o_ref):
    o_ref[...] = jnp.dot(a_ref[...], w_ref[...].astype(BF16), preferred_element_type=F32).astype(o_ref.dtype)


def _mm_relu2_kernel(a_ref, w_ref, o_ref):
    acc = jnp.dot(a_ref[...], w_ref[...].astype(BF16), preferred_element_type=F32)
    r = jnp.maximum(acc, 0.0)
    o_ref[...] = (r * r).astype(o_ref.dtype)


def _matmul(a, w, out_dtype, *, name, bm=MM_BM, bn=MM_BN, relu2=False):
    m, k = a.shape
    n = w.shape[1]
    return pl.pallas_call(
        _mm_relu2_kernel if relu2 else _mm_plain_kernel,
        grid=(m // bm, n // bn),
        in_specs=[pl.BlockSpec((bm, k), lambda i, j: (i, 0)), pl.BlockSpec((k, bn), lambda i, j: (0, j))],
        out_specs=pl.BlockSpec((bm, bn), lambda i, j: (i, j)),
        out_shape=jax.ShapeDtypeStruct((m, n), out_dtype),
        compiler_params=_params("parallel", "arbitrary"),
        name=name,
    )(a, w)


def _mm_kacc_kernel(a_ref, w_ref, o_ref, acc_ref):
    kk = pl.program_id(2)

    @pl.when(kk == 0)
    def _():
        acc_ref[...] = jnp.zeros_like(acc_ref)

    acc_ref[...] += jnp.dot(a_ref[...], w_ref[...].astype(BF16), preferred_element_type=F32)

    @pl.when(kk == pl.num_programs(2) - 1)
    def _():
        o_ref[...] = acc_ref[...].astype(o_ref.dtype)


def _matmul_ktiled(a, w, out_dtype, *, bk, name, bm=MM_BM, bn=MM_BN):
    m, k = a.shape
    n = w.shape[1]
    return pl.pallas_call(
        _mm_kacc_kernel,
        grid=(m // bm, n // bn, k // bk),
        in_specs=[pl.BlockSpec((bm, bk), lambda i, j, kk: (i, kk)), pl.BlockSpec((bk, bn), lambda i, j, kk: (kk, j))],
        out_specs=pl.BlockSpec((bm, bn), lambda i, j, kk: (i, j)),
        out_shape=jax.ShapeDtypeStruct((m, n), out_dtype),
        scratch_shapes=[pltpu.VMEM((bm, bn), F32)],
        compiler_params=_params("parallel", "parallel", "arbitrary"),
        name=name,
    )(a, w)


def _merge_kernel(c_ref, a_ref, wc_ref, wa_ref, ga_ref, gb_ref, o_ref):
    yc = jnp.dot(c_ref[...], wc_ref[...].astype(BF16), preferred_element_type=F32)
    ya = jnp.dot(a_ref[...], wa_ref[...].astype(BF16), preferred_element_type=F32)
    ga = jax.nn.sigmoid(ga_ref[...].astype(F32))
    gb = jax.nn.sigmoid(gb_ref[...].astype(F32))
    o_ref[...] = (ga * yc + gb * ya).astype(o_ref.dtype)


def _merge(conv_act, attn_act, wc, wa, gates_ab, *, bm=MM_BM, bn=MM_BN):
    m, kc = conv_act.shape
    ka = attn_act.shape[1]
    n = wc.shape[1]
    jb = n // bn
    return pl.pallas_call(
        _merge_kernel,
        grid=(m // bm, n // bn),
        in_specs=[
            pl.BlockSpec((bm, kc), lambda i, j: (i, 0)),
            pl.BlockSpec((bm, ka), lambda i, j: (i, 0)),
            pl.BlockSpec((kc, bn), lambda i, j: (0, j)),
            pl.BlockSpec((ka, bn), lambda i, j: (0, j)),
            pl.BlockSpec((bm, bn), lambda i, j: (i, j)),
            pl.BlockSpec((bm, bn), lambda i, j: (i, jb + j)),
        ],
        out_specs=pl.BlockSpec((bm, bn), lambda i, j: (i, j)),
        out_shape=jax.ShapeDtypeStruct((m, n), BF16),
        compiler_params=_params("parallel", "arbitrary"),
        name="gated_merge",
    )(conv_act, attn_act, wc, wa, gates_ab, gates_ab)


def _gelu_tanh(x):
    return 0.5 * x * (1.0 + jnp.tanh(np.sqrt(2.0 / np.pi) * (x + 0.044715 * (x * x * x))))


def _compress_kernel(xk_ref, xv_ref, pk_ref, pv_ref, w1k_ref, w1v_ref, w2k_ref, w2v_ref, ok_ref, ov_ref):
    for x_ref, p_ref, w1_ref, w2_ref, o_ref in ((xk_ref, pk_ref, w1k_ref, w2k_ref, ok_ref),
                                                (xv_ref, pv_ref, w1v_ref, w2v_ref, ov_ref)):
        x = x_ref[0, 0].astype(F32)
        first = jnp.dot((x + p_ref[0:1, :]).astype(BF16), w1_ref[0].astype(BF16), preferred_element_type=F32)
        second = jnp.dot((x + p_ref[1:2, :]).astype(BF16), w1_ref[1].astype(BF16), preferred_element_type=F32)
        h = first + pltpu.roll(second, N_CHUNK - 1, axis=0)
        o_ref[0, 0] = jnp.dot(_gelu_tanh(h).astype(BF16), w2_ref[...].astype(BF16),
                              preferred_element_type=F32).astype(o_ref.dtype)


def _compress(xk, xv, pk, pv, w1k, w1v, w2k, w2v):
    b = xk.shape[0]
    half = CMP_STRIDE * HEAD_DIM
    xspec = pl.BlockSpec((1, 1, N_CHUNK, half), lambda i, g: (i, g, 0, 0))
    pspec = pl.BlockSpec((2, half), lambda i, g: (0, 0))
    w1spec = pl.BlockSpec((2, half, CMP_HIDDEN), lambda i, g: (0, 0, 0))
    w2spec = pl.BlockSpec((CMP_HIDDEN, HEAD_DIM), lambda i, g: (0, 0))
    ospec = pl.BlockSpec((1, 1, N_CHUNK, HEAD_DIM), lambda i, g: (i, g, 0, 0))
    oshape = jax.ShapeDtypeStruct((b, N_KV_GROUPS, N_CHUNK, HEAD_DIM), BF16)
    return pl.pallas_call(
        _compress_kernel,
        grid=(b, N_KV_GROUPS),
        in_specs=[xspec, xspec, pspec, pspec, w1spec, w1spec, w2spec, w2spec],
        out_specs=[ospec, ospec],
        out_shape=[oshape, oshape],
        compiler_params=_params("parallel", "parallel"),
        name="compress_kv",
    )(xk, xv, pk, pv, w1k, w1v, w2k, w2v)


def _attn_kernel(q_ref, ks_ref, vs_ref, kw_ref, vw_ref, kc_ref, vc_ref, gl_ref, ovl_ref, blk_ref, o_ref,
                 s_ref, mx_ref, acc_ref, *, tq):
    tk = tq
    grp = pl.program_id(1)
    qi = pl.program_id(2)
    q0 = qi * tq
    nh = HEADS_PER_GROUP
    rows = nh * tq
    nt = (((1,), (1,)), ((), ()))

    q = q_ref[0]
    qs = jnp.concatenate([q[:, r * HEAD_DIM:(r + 1) * HEAD_DIM] for r in range(nh)], axis=0)

    s = lax.dot_general(qs, kc_ref[0, 0], nt, preferred_element_type=F32).reshape(nh, tq, N_CHUNK)
    n_idx = lax.broadcasted_iota(jnp.int32, (tq, N_CHUNK), 1)
    t_idx = q0 + lax.broadcasted_iota(jnp.int32, (tq, N_CHUNK), 0)
    vis = ((n_idx * CMP_STRIDE + (CMP_BLOCK - 1) <= t_idx) & (n_idx < N_CMP))[None]
    sm = jnp.where(vis, s, NEG_INF)
    mx = jnp.maximum(jnp.max(sm, axis=-1, keepdims=True), 0.5 * NEG_INF)
    e = jnp.exp2(sm - mx)
    p = e * (1.0 / jnp.maximum(jnp.sum(e, axis=-1, keepdims=True), 1e-30))
    o_cmp = jnp.dot(p.reshape(rows, N_CHUNK).astype(BF16), vc_ref[0, 0], preferred_element_type=F32)

    p_sum = p[0] + p[1] + p[2] + p[3]
    p_hi = p_sum.astype(BF16)
    rem = p_sum - p_hi.astype(F32)
    p_mid = rem.astype(BF16)
    p_lo = (rem - p_mid.astype(F32)).astype(BF16)
    ovl = ovl_ref[...]
    imp = (lax.dot_general(ovl, p_hi, nt, preferred_element_type=F32)
           + lax.dot_general(ovl, p_mid, nt, preferred_element_type=F32)
           + lax.dot_general(ovl, p_lo, nt, preferred_element_type=F32))

    j_idx = lax.broadcasted_iota(jnp.int32, (N_SLC, tq), 0)
    cur = jnp.right_shift(q0 + lax.broadcasted_iota(jnp.int32, (N_SLC, tq), 1), int(np.log2(SLC_BLOCK)))
    valid = j_idx <= cur
    forced = (j_idx == 0) | (j_idx == cur) | (j_idx == cur - 1)
    score = jnp.where(forced & valid, FORCED_SCORE, jnp.where(valid, imp, MASKED_SCORE))
    rank = jnp.zeros((N_SLC, tq), jnp.int32)
    for k in range(N_SLC):
        sk = score[k:k + 1, :]
        beats = (sk > score) | ((sk == score) & (j_idx > k))
        rank = rank + beats.astype(jnp.int32)
    unsel_t = jnp.where(rank < N_SELECT, 0.0, 1.0)
    unsel_t = jnp.concatenate([unsel_t, jnp.zeros((LANES - N_SLC, tq), F32)], axis=0)
    unsel = jnp.transpose(unsel_t).astype(BF16)

    q_slc = jnp.concatenate([qs, jnp.concatenate([unsel] * nh, axis=0)], axis=1)

    local_q = lax.broadcasted_iota(jnp.int32, (tq, tk), 0)
    local_k = lax.broadcasted_iota(jnp.int32, (tq, tk), 1)
    causal_bias = jnp.where(local_k <= local_q, 0.0, NEG_INF)
    far_bias = jnp.where(local_k > local_q, 0.0, NEG_INF)

    def pass1(qa, k_ref, with_blocks, kt0, slot0, biases, init):
        n = len(biases)
        k0 = pl.multiple_of(kt0 * tk, tk)
        k = k_ref[0, pl.ds(k0, n * tk), :]
        if with_blocks:
            k = jnp.concatenate([k, blk_ref[pl.ds(k0, n * tk), :]], axis=1)
        sc = lax.dot_general(qa, k, nt, preferred_element_type=F32)
        m = None
        for i, bias in enumerate(biases):
            part = sc[:, i * tk:(i + 1) * tk]
            if bias is not None:
                part = (part.reshape(nh, tq, tk) + bias[None]).reshape(rows, tk)
            s_ref[slot0 + i] = part
            for c in range(tk // LANES):
                piece = part[:, c * LANES:(c + 1) * LANES]
                m = piece if m is None else jnp.maximum(m, piece)
        mx_ref[...] = m if init else jnp.maximum(mx_ref[...], m)

    def pass2(v_ref, kt0, slot0, n, init):
        k0 = pl.multiple_of(kt0 * tk, tk)
        v_ext = jnp.concatenate([v_ref[0, pl.ds(k0, n * tk), :], jnp.ones((n * tk, LANES), BF16)], axis=1)
        m_rep = mx_ref[...]
        m_wide = jnp.concatenate([m_rep] * (tk // LANES), axis=1)
        pr = jnp.concatenate([jnp.exp2(s_ref[slot0 + i] - m_wide).astype(BF16) for i in range(n)], axis=1)
        d = jnp.dot(pr, v_ext, preferred_element_type=F32)
        if init:
            acc_ref[...] = d
        else:
            acc_ref[...] += d

    def spread_row_max():
        mx_ref[...] = jnp.broadcast_to(jnp.max(mx_ref[...], axis=-1, keepdims=True), (rows, LANES))

    pairs = jnp.right_shift(qi, 1)
    odd = jnp.bitwise_and(qi, 1) == 1

    pass1(q_slc, ks_ref, True, qi, qi, [causal_bias], True)

    def slc_pass1(i, carry):
        pass1(q_slc, ks_ref, True, 2 * i, 2 * i, [None, None], False)
        return carry

    lax.fori_loop(0, pairs, slc_pass1, 0)

    @pl.when(odd)
    def _():
        pass1(q_slc, ks_ref, True, qi - 1, qi - 1, [None], False)

    spread_row_max()
    pass2(vs_ref, qi, qi, 1, True)

    def slc_pass2(i, carry):
        pass2(vs_ref, 2 * i, 2 * i, 2, False)
        return carry

    lax.fori_loop(0, pairs, slc_pass2, 0)

    @pl.when(odd)
    def _():
        pass2(vs_ref, qi - 1, qi - 1, 1, False)

    acc = acc_ref[...]
    o_slc = acc[:, :HEAD_DIM] * (1.0 / acc[:, HEAD_DIM:])

    @pl.when(qi >= 2)
    def _():
        pass1(qs, kw_ref, False, qi - 2, 0, [far_bias, None, causal_bias], True)

    @pl.when(qi == 1)
    def _():
        pass1(qs, kw_ref, False, 0, 0, [None, causal_bias], True)

    @pl.when(qi == 0)
    def _():
        pass1(qs, kw_ref, False, 0, 0, [causal_bias], True)

    n_pad = jnp.maximum(WINDOW - 1 - (q0 + lax.broadcasted_iota(jnp.int32, (tq, LANES), 0)), 0).astype(F32)
    n_pad = jnp.concatenate([n_pad] * nh, axis=0)
    m_win = jnp.broadcast_to(jnp.max(mx_ref[...], axis=-1, keepdims=True), (rows, LANES))
    m_win = jnp.where(n_pad > 0.0, jnp.maximum(m_win, 0.0), m_win)
    mx_ref[...] = m_win

    @pl.when(qi >= 2)
    def _():
        pass2(vw_ref, qi - 2, 0, 3, True)

    @pl.when(qi == 1)
    def _():
        pass2(vw_ref, 0, 0, 2, True)

    @pl.when(qi == 0)
    def _():
        pass2(vw_ref, 0, 0, 1, True)

    acc = acc_ref[...]
    pad_term = n_pad * jnp.exp2(jnp.where(n_pad > 0.0, -m_win, 0.0))
    o_win = acc[:, :HEAD_DIM] * (1.0 / (acc[:, HEAD_DIM:] + pad_term))

    gates = jax.nn.sigmoid(gl_ref[0])
    gates = pltpu.roll(gates, (grp * (LANES - nh * N_NSA_BRANCHES)) % LANES, axis=1)
    for r in range(nh):
        sl = slice(r * tq, (r + 1) * tq)
        c = r * N_NSA_BRANCHES
        o_r = (gates[:, c:c + 1] * o_cmp[sl] + gates[:, c + 1:c + 2] * o_slc[sl] + gates[:, c + 2:c + 3] * o_win[sl])
        o_ref[0, :, r * HEAD_DIM:(r + 1) * HEAD_DIM] = o_r.astype(o_ref.dtype)


def _attention(proj3, gate_logits3, k_cmp, v_cmp, overlap_t, block_cols, *, tq=ATTN_TILE):
    b = proj3.shape[0]
    gw = HEADS_PER_GROUP * HEAD_DIM
    kv = lambda off: pl.BlockSpec((1, SEQ, HEAD_DIM), lambda i, g, t: (i, 0, off // HEAD_DIM + g))
    cmp_spec = pl.BlockSpec((1, 1, N_CHUNK, HEAD_DIM), lambda i, g, t: (i, g, 0, 0))
    rows = HEADS_PER_GROUP * tq
    return pl.pallas_call(
        functools.partial(_attn_kernel, tq=tq),
        grid=(b, N_KV_GROUPS, SEQ // tq),
        in_specs=[
            pl.BlockSpec((1, tq, gw), lambda i, g, t: (i, t, OFF_Q // gw + g)),
            kv(OFF_KS), kv(OFF_VS), kv(OFF_KW), kv(OFF_VW),
            cmp_spec, cmp_spec,
            pl.BlockSpec((1, tq, LANES), lambda i, g, t: (i, t, 0)),
            pl.BlockSpec((N_SLC, N_CHUNK), lambda i, g, t: (0, 0)),
            pl.BlockSpec((SEQ, LANES), lambda i, g, t: (0, 0)),
        ],
        out_specs=pl.BlockSpec((1, tq, gw), lambda i, g, t: (i, t, g)),
        out_shape=jax.ShapeDtypeStruct((b, SEQ, D_Q), BF16),
        scratch_shapes=[
            pltpu.VMEM((SEQ // tq, rows, tq), F32),
            pltpu.VMEM((rows, LANES), F32),
            pltpu.VMEM((rows, 2 * HEAD_DIM), F32),
        ],
        compiler_params=_params("parallel", "parallel", "arbitrary"),
        name="nsa_attention",
    )(proj3, proj3, proj3, proj3, proj3, k_cmp, v_cmp, gate_logits3, overlap_t, block_cols)


def _conv_kernel(a_ref, b_ref, ah_ref, bh_ref, w_ref, bias_ref, lg_ref, lb_ref, o_ref, u_ref, y_ref, *, ts):
    nc = D_CONV // LANES
    first = pl.program_id(1) == 0
    u_main = a_ref[0].astype(F32) * jax.nn.sigmoid(b_ref[0].astype(F32))
    u_halo = ah_ref[0].astype(F32) * jax.nn.sigmoid(bh_ref[0].astype(F32))
    u_halo = jnp.where(first, 0.0, u_halo)
    for c in range(nc):
        u_ref[c, 0:HALO, :] = u_halo[:, c * LANES:(c + 1) * LANES]
        u_ref[c, HALO:HALO + ts, :] = u_main[:, c * LANES:(c + 1) * LANES]

    rc = 64
    base = HALO - (CONV_WIDTH - 1)

    def chunk_body(c, carry):
        for r0 in range(0, ts, rc):
            acc = jnp.zeros((rc, LANES), F32)
            for j in range(CONV_WIDTH):
                acc = acc + u_ref[c, pl.ds(base + r0 + j, rc), :] * w_ref[c, j:j + 1, :]
            y_ref[c, r0:r0 + rc, :] = acc
        return carry

    lax.fori_loop(0, nc, chunk_body, 0)

    y = jnp.concatenate([y_ref[c] for c in range(nc)], axis=1) + bias_ref[...]
    mu = jnp.mean(y, axis=-1, keepdims=True)
    d = y - mu
    var = jnp.mean(d * d, axis=-1, keepdims=True)
    z = d * lax.rsqrt(var + LN_EPS) * lg_ref[...] + lb_ref[...]
    o_ref[0] = (z * jax.nn.sigmoid(z)).astype(o_ref.dtype)


def _conformer_conv(proj3, w_chunks, b_dw, ln_g, ln_b, *, ts=256):
    b = proj3.shape[0]
    nc = D_CONV // LANES
    per = ts // HALO
    main = lambda off: pl.BlockSpec((1, ts, D_CONV), lambda i, t: (i, t, off // D_CONV))
    halo = lambda off: pl.BlockSpec((1, HALO, D_CONV), lambda i, t: (i, jnp.maximum(t * per - 1, 0), off // D_CONV))
    vec = pl.BlockSpec((1, D_CONV), lambda i, t: (0, 0))
    return pl.pallas_call(
        functools.partial(_conv_kernel, ts=ts),
        grid=(b, SEQ // ts),
        in_specs=[main(OFF_GLU_A), main(OFF_GLU_B), halo(OFF_GLU_A), halo(OFF_GLU_B),
                  pl.BlockSpec((nc, HALO, LANES), lambda i, t: (0, 0, 0)), vec, vec, vec],
        out_specs=pl.BlockSpec((1, ts, D_CONV), lambda i, t: (i, t, 0)),
        out_shape=jax.ShapeDtypeStruct((b, SEQ, D_CONV), BF16),
        scratch_shapes=[pltpu.VMEM((nc, HALO + ts, LANES), F32), pltpu.VMEM((nc, ts, LANES), F32)],
        compiler_params=_params("parallel", "arbitrary"),
        name="conformer_conv",
    )(proj3, proj3, proj3, proj3, w_chunks, b_dw.reshape(1, D_CONV), ln_g.reshape(1, D_CONV), ln_b.reshape(1, D_CONV))


def _overlap_t():
    cmp_start = np.arange(N_CHUNK) * CMP_STRIDE
    slc_start = np.arange(N_SLC) * SLC_BLOCK
    ov = ((cmp_start[None, :] < slc_start[:, None] + SLC_BLOCK) & (cmp_start[None, :] + CMP_BLOCK > slc_start[:, None])
          & (np.arange(N_CHUNK)[None, :] < N_CMP))
    return jnp.asarray(ov.astype(np.float32), dtype=BF16)


def _block_cols():
    own = (np.arange(SEQ)[:, None] // SLC_BLOCK) == np.arange(LANES)[None, :]
    return jnp.asarray(np.where(own, -MASK_BIG, 0.0).astype(np.float32), dtype=BF16)


def _layer(x, norm_mix_pre, w_in, pos_cmp_k, w_cmp_k1, w_cmp_k2, pos_cmp_v, w_cmp_v1, w_cmp_v2,
           w_dw, b_dw, ln_conv_g, ln_conv_b, w_conv_out, w_attn_out, w_out, norm_mix_post,
           norm_mlp_pre, w_up, w_down, norm_mlp_post):
    b, s, d = x.shape
    m = b * s
    x2 = x.reshape(m, d)
    half = CMP_STRIDE * HEAD_DIM

    w_in_t = jnp.swapaxes(w_in, 0, 1)
    col_scale = jnp.ones((D_MAIN,), F32).at[OFF_Q:OFF_Q + D_Q].set(HEAD_DIM ** -0.5 * LOG2_E)

    u = _rmsnorm(x2, norm_mix_pre)
    proj = _matmul_nt(u, w_in_t, 0, D_MAIN, BF16, name="in_proj", col_scale=col_scale)
    gates_ab = _matmul_nt(u, w_in_t, D_MAIN + N_GATE, 2 * D_MODEL, BF16, name="in_proj_merge_gates")
    gate_logits = _matmul_nt(u, w_in_t, D_MAIN, LANES, F32, bn=LANES, name="in_proj_branch_gates")
    proj3 = proj.reshape(b, s, D_MAIN)

    def chunks(off):
        t = proj3[:, :, off:off + D_KV].reshape(b, N_CHUNK, CMP_STRIDE, N_KV_GROUPS, HEAD_DIM)
        return t.transpose(0, 3, 1, 2, 4).reshape(b, N_KV_GROUPS, N_CHUNK, half)

    k_cmp, v_cmp = _compress(
        chunks(OFF_KC), chunks(OFF_VC), pos_cmp_k.reshape(2, half), pos_cmp_v.reshape(2, half),
        w_cmp_k1.reshape(2, half, CMP_HIDDEN), w_cmp_v1.reshape(2, half, CMP_HIDDEN), w_cmp_k2, w_cmp_v2)

    attn = _attention(proj3, gate_logits.reshape(b, s, LANES), k_cmp, v_cmp, _overlap_t(), _block_cols())

    w_chunks = jnp.pad(w_dw.reshape(CONV_WIDTH, D_CONV), ((0, HALO - CONV_WIDTH), (0, 0)))
    w_chunks = w_chunks.reshape(HALO, D_CONV // LANES, LANES).transpose(1, 0, 2)
    conv = _conformer_conv(proj3, w_chunks, b_dw, ln_conv_g, ln_conv_b)

    merged = _merge(conv.reshape(m, D_CONV), attn.reshape(m, D_Q), w_conv_out, w_attn_out, gates_ab)
    z = _matmul(merged, w_out, BF16, name="out_proj")
    x1, h = _post_mix(x2, z, norm_mix_post, norm_mlp_pre)

    hidden = _matmul(h, w_up, BF16, name="mlp_up", relu2=True)
    y = _matmul_ktiled(hidden, w_down, BF16, bm=2048, bn=1024, bk=1024, name="mlp_down")
    return _post_mlp(x1, y, norm_mlp_post).reshape(b, s, d)


def kernel(x, norm_mix_pre, w_in, pos_cmp_k, w_cmp_k1, w_cmp_k2, pos_cmp_v, w_cmp_v1, w_cmp_v2, w_dw, b_dw,
           ln_conv_g, ln_conv_b, w_conv_out, w_attn_out, w_out, norm_mix_post, norm_mlp_pre, w_up, w_down,
           norm_mlp_post):
    for l in range(norm_mix_pre.shape[0]):
        x = _layer(x, norm_mix_pre[l], w_in[l], pos_cmp_k[l], w_cmp_k1[l], w_cmp_k2[l], pos_cmp_v[l], w_cmp_v1[l],
                   w_cmp_v2[l], w_dw[l], b_dw[l], ln_conv_g[l], ln_conv_b[l], w_conv_out[l], w_attn_out[l], w_out[l],
                   norm_mix_post[l], norm_mlp_pre[l], w_up[l], w_down[l], norm_mlp_post[l])
    return x
```

```python
import functools

import numpy as np
import jax
import jax.numpy as jnp
from jax import lax
from jax.experimental import pallas as pl
from jax.experimental.pallas import tpu as pltpu

D_MODEL = 4096
SEQ = 2048
D_CONV = D_MODEL // 2
CONV_WIDTH = 31
HEAD_DIM = 128
N_HEADS = 16
N_KV_GROUPS = 4
HEADS_PER_GROUP = N_HEADS // N_KV_GROUPS
CMP_BLOCK = 32
CMP_STRIDE = 16
CMP_HIDDEN = 2 * HEAD_DIM
SLC_BLOCK = 64
N_SELECT = 16
WINDOW = 512
N_NSA_BRANCHES = 3
D_FF = 4 * D_MODEL
D_Q = N_HEADS * HEAD_DIM
D_KV = N_KV_GROUPS * HEAD_DIM
N_GATE = N_NSA_BRANCHES * N_HEADS
D_MAIN = 2 * D_CONV + D_Q + 6 * D_KV
N_CHUNK = SEQ // CMP_STRIDE
N_CMP = N_CHUNK - CMP_BLOCK // CMP_STRIDE + 1
N_SLC = SEQ // SLC_BLOCK

NORM_EPS = 1e-6
LN_EPS = 1e-5
FORCED_SCORE = 1e4
MASKED_SCORE = -1e4
NEG_INF = -1e30
MASK_BIG = 2.0 ** 100
LOG2_E = float(np.log2(np.e))

LANES = 128
SUBLANES = 8
HALO = 32
ATTN_TILE = 256
MM_BM = 1024
MM_BN = 512
VMEM_LIMIT = 56 * 1024 * 1024

OFF_GLU_A = 0
OFF_GLU_B = D_CONV
OFF_Q = 2 * D_CONV
OFF_KC = OFF_Q + D_Q
OFF_VC = OFF_KC + D_KV
OFF_KS = OFF_VC + D_KV
OFF_VS = OFF_KS + D_KV
OFF_KW = OFF_VS + D_KV
OFF_VW = OFF_KW + D_KV

F32 = jnp.float32
BF16 = jnp.bfloat16


def _params(*sem):
    return pltpu.CompilerParams(dimension_semantics=sem, vmem_limit_bytes=VMEM_LIMIT)


def _rmsnorm_gates_kernel(x_ref, g_ref, wt_ref, u_ref, o_ref):
    x = x_ref[...]
    y = x * lax.rsqrt(jnp.mean(x * x, axis=-1, keepdims=True) + NORM_EPS)
    u = (y * g_ref[...]).astype(u_ref.dtype)
    u_ref[...] = u
    o_ref[...] = lax.dot_general(u, wt_ref[...].astype(BF16), (((1,), (1,)), ((), ())), preferred_element_type=F32)


def _rmsnorm_and_branch_gates(x, g, wt, row0, bm=512):
    m, d = x.shape
    return pl.pallas_call(
        _rmsnorm_gates_kernel,
        grid=(m // bm,),
        in_specs=[pl.BlockSpec((bm, d), lambda i: (i, 0)), pl.BlockSpec((1, d), lambda i: (0, 0)),
                  pl.BlockSpec((pl.Element(LANES), pl.Element(d)), lambda i: (row0, 0))],
        out_specs=[pl.BlockSpec((bm, d), lambda i: (i, 0)), pl.BlockSpec((bm, LANES), lambda i: (i, 0))],
        out_shape=[jax.ShapeDtypeStruct((m, d), BF16), jax.ShapeDtypeStruct((m, LANES), F32)],
        compiler_params=_params("parallel"),
        name="rmsnorm_in_branch_gates",
    )(x, g.reshape(1, d), wt)


def _post_mix_kernel(x_ref, z_ref, g1_ref, g2_ref, x1_ref, h_ref):
    z = z_ref[...].astype(F32)
    zn = z * lax.rsqrt(jnp.mean(z * z, axis=-1, keepdims=True) + NORM_EPS)
    x1 = x_ref[...] + zn * g1_ref[...]
    x1_ref[...] = x1
    hn = x1 * lax.rsqrt(jnp.mean(x1 * x1, axis=-1, keepdims=True) + NORM_EPS)
    h_ref[...] = (hn * g2_ref[...]).astype(h_ref.dtype)


def _post_mix(x, z, g1, g2, bm=256):
    m, d = x.shape
    row = pl.BlockSpec((bm, d), lambda i: (i, 0))
    vec = pl.BlockSpec((1, d), lambda i: (0, 0))
    return pl.pallas_call(
        _post_mix_kernel,
        grid=(m // bm,),
        in_specs=[row, row, vec, vec],
        out_specs=[row, row],
        out_shape=[jax.ShapeDtypeStruct((m, d), F32), jax.ShapeDtypeStruct((m, d), BF16)],
        compiler_params=_params("parallel"),
        name="post_mix_norm",
    )(x, z, g1.reshape(1, d), g2.reshape(1, d))


def _post_mlp_kernel(x_ref, y_ref, g_ref, o_ref):
    y = y_ref[...].astype(F32)
    yn = y * lax.rsqrt(jnp.mean(y * y, axis=-1, keepdims=True) + NORM_EPS)
    o_ref[...] = x_ref[...] + yn * g_ref[...]


def _post_mlp(x, y, g, bm=512):
    m, d = x.shape
    row = pl.BlockSpec((bm, d), lambda i: (i, 0))
    return pl.pallas_call(
        _post_mlp_kernel,
        grid=(m // bm,),
        in_specs=[row, row, pl.BlockSpec((1, d), lambda i: (0, 0))],
        out_specs=row,
        out_shape=jax.ShapeDtypeStruct((m, d), F32),
        compiler_params=_params("parallel"),
        name="post_mlp_norm",
    )(x, y, g.reshape(1, d))


_NT_DIMS = (((1,), (1,)), ((), ()))


def _mm_nt_scale_kernel(a_ref, wt_ref, s_ref, o_ref):
    acc = lax.dot_general(a_ref[...], wt_ref[...].astype(BF16), _NT_DIMS, preferred_element_type=F32)
    o_ref[...] = (acc * s_ref[...]).astype(o_ref.dtype)


def _mm_nt_kernel(a_ref, wt_ref, o_ref):
    acc = lax.dot_general(a_ref[...], wt_ref[...].astype(BF16), _NT_DIMS, preferred_element_type=F32)
    o_ref[...] = acc.astype(o_ref.dtype)


def _matmul_nt(a, wt, row0, n, out_dtype, *, name, bm=MM_BM, bn=MM_BN, col_scale=None):
    m, k = a.shape
    in_specs = [pl.BlockSpec((bm, k), lambda i, j: (i, 0)),
                pl.BlockSpec((pl.Element(bn), pl.Element(k)),
                             lambda i, j: ((row0 // SUBLANES + j * (bn // SUBLANES)) * SUBLANES, 0))]
    args = [a, wt]
    body = _mm_nt_kernel
    if col_scale is not None:
        body = _mm_nt_scale_kernel
        in_specs.append(pl.BlockSpec((1, bn), lambda i, j: (0, j)))
        args.append(col_scale.reshape(1, n))
    return pl.pallas_call(
        body,
        grid=(m // bm, n // bn),
        in_specs=in_specs,
        out_specs=pl.BlockSpec((bm, bn), lambda i, j: (i, j)),
        out_shape=jax.ShapeDtypeStruct((m, n), out_dtype),
        compiler_params=_params("parallel", "arbitrary"),
        name=name,
    )(*args)


def _mm_plain_kernel(a_ref, w_ref, o_ref):
    o_ref[...] = jnp.dot(a_ref[...], w_ref[...].astype(BF16), preferred_element_type=F32).astype(o_ref.dtype)


def _mm_relu2_kernel(a_ref, w_ref, o_ref):
    acc = jnp.dot(a_ref[...], w_ref[...].astype(BF16), preferred_element_type=F32)
    r = jnp.maximum(acc, 0.0)
    o_ref[...] = (r * r).astype(o_ref.dtype)


def _matmul(a, w, out_dtype, *, name, bm=MM_BM, bn=MM_BN, relu2=False):
    m, k = a.shape
    n = w.shape[1]
    return pl.pallas_call(
        _mm_relu2_kernel if relu2 else _mm_plain_kernel,
        grid=(m // bm, n // bn),
        in_specs=[pl.BlockSpec((bm, k), lambda i, j: (i, 0)), pl.BlockSpec((k, bn), lambda i, j: (0, j))],
        out_specs=pl.BlockSpec((bm, bn), lambda i, j: (i, j)),
        out_shape=jax.ShapeDtypeStruct((m, n), out_dtype),
        compiler_params=_params("parallel", "arbitrary"),
        name=name,
    )(a, w)


def _mm_kacc_kernel(a_ref, w_ref, o_ref, acc_ref):
    kk = pl.program_id(2)

    @pl.when(kk == 0)
    def _():
        acc_ref[...] = jnp.zeros_like(acc_ref)

    acc_ref[...] += jnp.dot(a_ref[...], w_ref[...].astype(BF16), preferred_element_type=F32)

    @pl.when(kk == pl.num_programs(2) - 1)
    def _():
        o_ref[...] = acc_ref[...].astype(o_ref.dtype)


def _matmul_ktiled(a, w, out_dtype, *, bk, name, bm=MM_BM, bn=MM_BN):
    m, k = a.shape
    n = w.shape[1]
    return pl.pallas_call(
        _mm_kacc_kernel,
        grid=(m // bm, n // bn, k // bk),
        in_specs=[pl.BlockSpec((bm, bk), lambda i, j, kk: (i, kk)), pl.BlockSpec((bk, bn), lambda i, j, kk: (kk, j))],
        out_specs=pl.BlockSpec((bm, bn), lambda i, j, kk: (i, j)),
        out_shape=jax.ShapeDtypeStruct((m, n), out_dtype),
        scratch_shapes=[pltpu.VMEM((bm, bn), F32)],
        compiler_params=_params("parallel", "parallel", "arbitrary"),
        name=name,
    )(a, w)


def _merge_kernel(c_ref, a_ref, wc_ref, wa_ref, ga_ref, gb_ref, o_ref):
    yc = jnp.dot(c_ref[...], wc_ref[...].astype(BF16), preferred_element_type=F32)
    ya = jnp.dot(a_ref[...], wa_ref[...].astype(BF16), preferred_element_type=F32)
    ga = jax.nn.sigmoid(ga_ref[...].astype(F32))
    gb = jax.nn.sigmoid(gb_ref[...].astype(F32))
    o_ref[...] = (ga * yc + gb * ya).astype(o_ref.dtype)


def _merge(conv_act, attn_act, wc, wa, gates_ab, *, bm=MM_BM, bn=MM_BN):
    m, kc = conv_act.shape
    ka = attn_act.shape[1]
    n = wc.shape[1]
    jb = n // bn
    return pl.pallas_call(
        _merge_kernel,
        grid=(m // bm, n // bn),
        in_specs=[
            pl.BlockSpec((bm, kc), lambda i, j: (i, 0)),
            pl.BlockSpec((bm, ka), lambda i, j: (i, 0)),
            pl.BlockSpec((kc, bn), lambda i, j: (0, j)),
            pl.BlockSpec((ka, bn), lambda i, j: (0, j)),
            pl.BlockSpec((bm, bn), lambda i, j: (i, j)),
            pl.BlockSpec((bm, bn), lambda i, j: (i, jb + j)),
        ],
        out_specs=pl.BlockSpec((bm, bn), lambda i, j: (i, j)),
        out_shape=jax.ShapeDtypeStruct((m, n), BF16),
        compiler_params=_params("parallel", "arbitrary"),
        name="gated_merge",
    )(conv_act, attn_act, wc, wa, gates_ab, gates_ab)


def _gelu_tanh(x):
    return 0.5 * x * (1.0 + jnp.tanh(np.sqrt(2.0 / np.pi) * (x + 0.044715 * (x * x * x))))


def _compress_kernel(xk_ref, xv_ref, pk_ref, pv_ref, w1k_ref, w1v_ref, w2k_ref, w2v_ref, ok_ref, ov_ref):
    for x_ref, p_ref, w1_ref, w2_ref, o_ref in ((xk_ref, pk_ref, w1k_ref, w2k_ref, ok_ref),
                                                (xv_ref, pv_ref, w1v_ref, w2v_ref, ov_ref)):
        x = x_ref[0, 0].astype(F32)
        first = jnp.dot((x + p_ref[0:1, :]).astype(BF16), w1_ref[0].astype(BF16), preferred_element_type=F32)
        second = jnp.dot((x + p_ref[1:2, :]).astype(BF16), w1_ref[1].astype(BF16), preferred_element_type=F32)
        h = first + pltpu.roll(second, N_CHUNK - 1, axis=0)
        o_ref[0, 0] = jnp.dot(_gelu_tanh(h).astype(BF16), w2_ref[...].astype(BF16),
                              preferred_element_type=F32).astype(o_ref.dtype)


def _compress(xk, xv, pk, pv, w1k, w1v, w2k, w2v):
    b = xk.shape[0]
    half = CMP_STRIDE * HEAD_DIM
    xspec = pl.BlockSpec((1, 1, N_CHUNK, half), lambda i, g: (i, g, 0, 0))
    pspec = pl.BlockSpec((2, half), lambda i, g: (0, 0))
    w1spec = pl.BlockSpec((2, half, CMP_HIDDEN), lambda i, g: (0, 0, 0))
    w2spec = pl.BlockSpec((CMP_HIDDEN, HEAD_DIM), lambda i, g: (0, 0))
    ospec = pl.BlockSpec((1, 1, N_CHUNK, HEAD_DIM), lambda i, g: (i, g, 0, 0))
    oshape = jax.ShapeDtypeStruct((b, N_KV_GROUPS, N_CHUNK, HEAD_DIM), BF16)
    return pl.pallas_call(
        _compress_kernel,
        grid=(b, N_KV_GROUPS),
        in_specs=[xspec, xspec, pspec, pspec, w1spec, w1spec, w2spec, w2spec],
        out_specs=[ospec, ospec],
        out_shape=[oshape, oshape],
        compiler_params=_params("parallel", "parallel"),
        name="compress_kv",
    )(xk, xv, pk, pv, w1k, w1v, w2k, w2v)


def _attn_kernel(q_ref, ks_ref, vs_ref, kw_ref, vw_ref, kc_ref, vc_ref, gl_ref, ovl_ref, blk_ref, o_ref,
                 s_ref, mx_ref, acc_ref, sw_ref, ow_ref, *, tq):
    tk = tq
    grp = pl.program_id(1)
    qi = pl.program_id(2)
    q0 = qi * tq
    nh = HEADS_PER_GROUP
    rows = nh * tq
    nt = (((1,), (1,)), ((), ()))

    q = q_ref[0]
    qs = jnp.concatenate([q[:, r * HEAD_DIM:(r + 1) * HEAD_DIM] for r in range(nh)], axis=0)

    n_win = WINDOW // tk + 1
    wk0 = pl.multiple_of(jnp.maximum(qi - (n_win - 1), 0) * tk, tk)
    sc = lax.dot_general(qs, kw_ref[0, pl.ds(wk0, n_win * tk), :], nt, preferred_element_type=F32)
    key = wk0 + lax.broadcasted_iota(jnp.int32, (tq, n_win * tk), 1)
    t_abs = q0 + lax.broadcasted_iota(jnp.int32, (tq, n_win * tk), 0)
    win_bias = jnp.where((key <= t_abs) & (key > t_abs - WINDOW), 0.0, NEG_INF)
    m_win = None
    for i in range(n_win):
        part = sc[:, i * tk:(i + 1) * tk].reshape(nh, tq, tk) + win_bias[None, :, i * tk:(i + 1) * tk]
        part = part.reshape(rows, tk)
        sw_ref[i] = part
        for c in range(tk // LANES):
            piece = part[:, c * LANES:(c + 1) * LANES]
            m_win = piece if m_win is None else jnp.maximum(m_win, piece)
    n_pad = jnp.maximum(WINDOW - 1 - (q0 + lax.broadcasted_iota(jnp.int32, (tq, LANES), 0)), 0).astype(F32)
    n_pad = jnp.concatenate([n_pad] * nh, axis=0)
    m_win = jnp.broadcast_to(jnp.max(m_win, axis=-1, keepdims=True), (rows, LANES))
    m_win = jnp.where(n_pad > 0.0, jnp.maximum(m_win, 0.0), m_win)
    m_wide = jnp.concatenate([m_win] * (tk // LANES), axis=1)
    pr = jnp.concatenate([jnp.exp2(sw_ref[i] - m_wide).astype(BF16) for i in range(n_win)], axis=1)
    v_ext = jnp.concatenate([vw_ref[0, pl.ds(wk0, n_win * tk), :], jnp.ones((n_win * tk, LANES), BF16)], axis=1)
    acc_w = jnp.dot(pr, v_ext, preferred_element_type=F32)
    pad_term = n_pad * jnp.exp2(jnp.where(n_pad > 0.0, -m_win, 0.0))
    ow_ref[...] = acc_w[:, :HEAD_DIM] * (1.0 / (acc_w[:, HEAD_DIM:] + pad_term))

    s = lax.dot_general(qs, kc_ref[0, 0], nt, preferred_element_type=F32).reshape(nh, tq, N_CHUNK)
    n_idx = lax.broadcasted_iota(jnp.int32, (tq, N_CHUNK), 1)
    t_idx = q0 + lax.broadcasted_iota(jnp.int32, (tq, N_CHUNK), 0)
    vis = ((n_idx * CMP_STRIDE + (CMP_BLOCK - 1) <= t_idx) & (n_idx < N_CMP))[None]
    sm = jnp.where(vis, s, NEG_INF)
    mx = jnp.maximum(jnp.max(sm, axis=-1, keepdims=True), 0.5 * NEG_INF)
    e = jnp.exp2(sm - mx)
    p = e * (1.0 / jnp.maximum(jnp.sum(e, axis=-1, keepdims=True), 1e-30))
    o_cmp = jnp.dot(p.reshape(rows, N_CHUNK).astype(BF16), vc_ref[0, 0], preferred_element_type=F32)

    p_sum = p[0] + p[1] + p[2] + p[3]
    p_hi = p_sum.astype(BF16)
    rem = p_sum - p_hi.astype(F32)
    p_mid = rem.astype(BF16)
    p_lo = (rem - p_mid.astype(F32)).astype(BF16)
    ovl = ovl_ref[...]
    imp = (lax.dot_general(ovl, p_hi, nt, preferred_element_type=F32)
           + lax.dot_general(ovl, p_mid, nt, preferred_element_type=F32)
           + lax.dot_general(ovl, p_lo, nt, preferred_element_type=F32))

    j_idx = lax.broadcasted_iota(jnp.int32, (N_SLC, tq), 0)
    cur = jnp.right_shift(q0 + lax.broadcasted_iota(jnp.int32, (N_SLC, tq), 1), int(np.log2(SLC_BLOCK)))
    valid = j_idx <= cur
    forced = (j_idx == 0) | (j_idx == cur) | (j_idx == cur - 1)
    score = jnp.where(forced & valid, FORCED_SCORE, jnp.where(valid, imp, MASKED_SCORE))
    rank = jnp.zeros((N_SLC, tq), jnp.int32)
    for k in range(N_SLC):
        sk = score[k:k + 1, :]
        beats = (sk > score) | ((sk == score) & (j_idx > k))
        rank = rank + beats.astype(jnp.int32)
    unsel_t = jnp.where(rank < N_SELECT, 0.0, 1.0)
    unsel_t = jnp.concatenate([unsel_t, jnp.zeros((LANES - N_SLC, tq), F32)], axis=0)
    unsel = jnp.transpose(unsel_t).astype(BF16)

    q_slc = jnp.concatenate([qs, jnp.concatenate([unsel] * nh, axis=0)], axis=1)

    local_q = lax.broadcasted_iota(jnp.int32, (tq, tk), 0)
    local_k = lax.broadcasted_iota(jnp.int32, (tq, tk), 1)
    causal_bias = jnp.where(local_k <= local_q, 0.0, NEG_INF)

    def pass1(qa, k_ref, with_blocks, kt0, slot0, biases, init):
        n = len(biases)
        k0 = pl.multiple_of(kt0 * tk, tk)
        k = k_ref[0, pl.ds(k0, n * tk), :]
        if with_blocks:
            k = jnp.concatenate([k, blk_ref[pl.ds(k0, n * tk), :]], axis=1)
        sc = lax.dot_general(qa, k, nt, preferred_element_type=F32)
        m = None
        for i, bias in enumerate(biases):
            part = sc[:, i * tk:(i + 1) * tk]
            if bias is not None:
                part = (part.reshape(nh, tq, tk) + bias[None]).reshape(rows, tk)
            s_ref[slot0 + i] = part
            for c in range(tk // LANES):
                piece = part[:, c * LANES:(c + 1) * LANES]
                m = piece if m is None else jnp.maximum(m, piece)
        mx_ref[...] = m if init else jnp.maximum(mx_ref[...], m)

    def pass2(v_ref, kt0, slot0, n, init):
        k0 = pl.multiple_of(kt0 * tk, tk)
        v_ext = jnp.concatenate([v_ref[0, pl.ds(k0, n * tk), :], jnp.ones((n * tk, LANES), BF16)], axis=1)
        m_rep = mx_ref[...]
        m_wide = jnp.concatenate([m_rep] * (tk // LANES), axis=1)
        pr = jnp.concatenate([jnp.exp2(s_ref[slot0 + i] - m_wide).astype(BF16) for i in range(n)], axis=1)
        d = jnp.dot(pr, v_ext, preferred_element_type=F32)
        if init:
            acc_ref[...] = d
        else:
            acc_ref[...] += d

    def spread_row_max():
        mx_ref[...] = jnp.broadcast_to(jnp.max(mx_ref[...], axis=-1, keepdims=True), (rows, LANES))

    pairs = jnp.right_shift(qi, 1)
    odd = jnp.bitwise_and(qi, 1) == 1

    pass1(q_slc, ks_ref, True, qi, qi, [causal_bias], True)

    def slc_pass1(i, carry):
        pass1(q_slc, ks_ref, True, 2 * i, 2 * i, [None, None], False)
        return carry

    lax.fori_loop(0, pairs, slc_pass1, 0)

    @pl.when(odd)
    def _():
        pass1(q_slc, ks_ref, True, qi - 1, qi - 1, [None], False)

    spread_row_max()
    pass2(vs_ref, qi, qi, 1, True)

    def slc_pass2(i, carry):
        pass2(vs_ref, 2 * i, 2 * i, 2, False)
        return carry

    lax.fori_loop(0, pairs, slc_pass2, 0)

    @pl.when(odd)
    def _():
        pass2(vs_ref, qi - 1, qi - 1, 1, False)

    acc = acc_ref[...]
    o_slc = acc[:, :HEAD_DIM] * (1.0 / acc[:, HEAD_DIM:])

    o_win = ow_ref[...]

    gates = jax.nn.sigmoid(gl_ref[0])
    gates = pltpu.roll(gates, (grp * (LANES - nh * N_NSA_BRANCHES)) % LANES, axis=1)
    for r in range(nh):
        sl = slice(r * tq, (r + 1) * tq)
        c = r * N_NSA_BRANCHES
        o_r = (gates[:, c:c + 1] * o_cmp[sl] + gates[:, c + 1:c + 2] * o_slc[sl] + gates[:, c + 2:c + 3] * o_win[sl])
        o_ref[0, :, r * HEAD_DIM:(r + 1) * HEAD_DIM] = o_r.astype(o_ref.dtype)


def _attention(proj3, gate_logits3, k_cmp, v_cmp, overlap_t, block_cols, *, tq=ATTN_TILE):
    b = proj3.shape[0]
    gw = HEADS_PER_GROUP * HEAD_DIM
    kv = lambda off: pl.BlockSpec((1, SEQ, HEAD_DIM), lambda i, g, t: (i, 0, off // HEAD_DIM + g))
    cmp_spec = pl.BlockSpec((1, 1, N_CHUNK, HEAD_DIM), lambda i, g, t: (i, g, 0, 0))
    rows = HEADS_PER_GROUP * tq
    return pl.pallas_call(
        functools.partial(_attn_kernel, tq=tq),
        grid=(b, N_KV_GROUPS, SEQ // tq),
        in_specs=[
            pl.BlockSpec((1, tq, gw), lambda i, g, t: (i, t, OFF_Q // gw + g)),
            kv(OFF_KS), kv(OFF_VS), kv(OFF_KW), kv(OFF_VW),
            cmp_spec, cmp_spec,
            pl.BlockSpec((1, tq, LANES), lambda i, g, t: (i, t, 0)),
            pl.BlockSpec((N_SLC, N_CHUNK), lambda i, g, t: (0, 0)),
            pl.BlockSpec((SEQ, LANES), lambda i, g, t: (0, 0)),
        ],
        out_specs=pl.BlockSpec((1, tq, gw), lambda i, g, t: (i, t, g)),
        out_shape=jax.ShapeDtypeStruct((b, SEQ, D_Q), BF16),
        scratch_shapes=[
            pltpu.VMEM((SEQ // tq, rows, tq), F32),
            pltpu.VMEM((rows, LANES), F32),
            pltpu.VMEM((rows, 2 * HEAD_DIM), F32),
            pltpu.VMEM((WINDOW // tq + 1, rows, tq), F32),
            pltpu.VMEM((rows, HEAD_DIM), F32),
        ],
        compiler_params=_params("parallel", "parallel", "arbitrary"),
        name="nsa_attention",
    )(proj3, proj3, proj3, proj3, proj3, k_cmp, v_cmp, gate_logits3, overlap_t, block_cols)


def _conv_kernel(a_ref, b_ref, ah_ref, bh_ref, w_ref, bias_ref, lg_ref, lb_ref, o_ref, u_ref, y_ref, *, ts):
    nc = D_CONV // LANES
    first = pl.program_id(1) == 0
    u_main = a_ref[0].astype(F32) * jax.nn.sigmoid(b_ref[0].astype(F32))
    u_halo = ah_ref[0].astype(F32) * jax.nn.sigmoid(bh_ref[0].astype(F32))
    u_halo = jnp.where(first, 0.0, u_halo)
    for c in range(nc):
        u_ref[c, 0:HALO, :] = u_halo[:, c * LANES:(c + 1) * LANES]
        u_ref[c, HALO:HALO + ts, :] = u_main[:, c * LANES:(c + 1) * LANES]

    rc = 64
    base = HALO - (CONV_WIDTH - 1)

    def chunk_body(c, carry):
        for r0 in range(0, ts, rc):
            acc = jnp.zeros((rc, LANES), F32)
            for j in range(CONV_WIDTH):
                acc = acc + u_ref[c, pl.ds(base + r0 + j, rc), :] * w_ref[c, j:j + 1, :]
            y_ref[c, r0:r0 + rc, :] = acc
        return carry

    lax.fori_loop(0, nc, chunk_body, 0)

    y = jnp.concatenate([y_ref[c] for c in range(nc)], axis=1) + bias_ref[...]
    mu = jnp.mean(y, axis=-1, keepdims=True)
    d = y - mu
    var = jnp.mean(d * d, axis=-1, keepdims=True)
    z = d * lax.rsqrt(var + LN_EPS) * lg_ref[...] + lb_ref[...]
    o_ref[0] = (z * jax.nn.sigmoid(z)).astype(o_ref.dtype)


def _conformer_conv(proj3, w_chunks, b_dw, ln_g, ln_b, *, ts=256):
    b = proj3.shape[0]
    nc = D_CONV // LANES
    per = ts // HALO
    main = lambda off: pl.BlockSpec((1, ts, D_CONV), lambda i, t: (i, t, off // D_CONV))
    halo = lambda off: pl.BlockSpec((1, HALO, D_CONV), lambda i, t: (i, jnp.maximum(t * per - 1, 0), off // D_CONV))
    vec = pl.BlockSpec((1, D_CONV), lambda i, t: (0, 0))
    return pl.pallas_call(
        functools.partial(_conv_kernel, ts=ts),
        grid=(b, SEQ // ts),
        in_specs=[main(OFF_GLU_A), main(OFF_GLU_B), halo(OFF_GLU_A), halo(OFF_GLU_B),
                  pl.BlockSpec((nc, HALO, LANES), lambda i, t: (0, 0, 0)), vec, vec, vec],
        out_specs=pl.BlockSpec((1, ts, D_CONV), lambda i, t: (i, t, 0)),
        out_shape=jax.ShapeDtypeStruct((b, SEQ, D_CONV), BF16),
        scratch_shapes=[pltpu.VMEM((nc, HALO + ts, LANES), F32), pltpu.VMEM((nc, ts, LANES), F32)],
        compiler_params=_params("parallel", "arbitrary"),
        name="conformer_conv",
    )(proj3, proj3, proj3, proj3, w_chunks, b_dw.reshape(1, D_CONV), ln_g.reshape(1, D_CONV), ln_b.reshape(1, D_CONV))


def _overlap_t():
    cmp_start = np.arange(N_CHUNK) * CMP_STRIDE
    slc_start = np.arange(N_SLC) * SLC_BLOCK
    ov = ((cmp_start[None, :] < slc_start[:, None] + SLC_BLOCK) & (cmp_start[None, :] + CMP_BLOCK > slc_start[:, None])
          & (np.arange(N_CHUNK)[None, :] < N_CMP))
    return jnp.asarray(ov.astype(np.float32), dtype=BF16)


def _block_cols():
    own = (np.arange(SEQ)[:, None] // SLC_BLOCK) == np.arange(LANES)[None, :]
    return jnp.asarray(np.where(own, -MASK_BIG, 0.0).astype(np.float32), dtype=BF16)


def _layer(x, norm_mix_pre, w_in, pos_cmp_k, w_cmp_k1, w_cmp_k2, pos_cmp_v, w_cmp_v1, w_cmp_v2,
           w_dw, b_dw, ln_conv_g, ln_conv_b, w_conv_out, w_attn_out, w_out, norm_mix_post,
           norm_mlp_pre, w_up, w_down, norm_mlp_post):
    b, s, d = x.shape
    m = b * s
    x2 = x.reshape(m, d)
    half = CMP_STRIDE * HEAD_DIM

    w_in_t = jnp.swapaxes(w_in, 0, 1)
    col_scale = jnp.ones((D_MAIN,), F32).at[OFF_Q:OFF_Q + D_Q].set(HEAD_DIM ** -0.5 * LOG2_E)

    u, gate_logits = _rmsnorm_and_branch_gates(x2, norm_mix_pre, w_in_t, D_MAIN)
    proj = _matmul_nt(u, w_in_t, 0, D_MAIN, BF16, name="in_proj", col_scale=col_scale)
    proj3 = proj.reshape(b, s, D_MAIN)

    w_chunks = jnp.pad(w_dw.reshape(CONV_WIDTH, D_CONV), ((0, HALO - CONV_WIDTH), (0, 0)))
    w_chunks = w_chunks.reshape(HALO, D_CONV // LANES, LANES).transpose(1, 0, 2)
    gates_ab = _matmul_nt(u, w_in_t, D_MAIN + N_GATE, 2 * D_MODEL, BF16, name="in_proj_merge_gates")
    conv = _conformer_conv(proj3, w_chunks, b_dw, ln_conv_g, ln_conv_b)

    def chunks(off):
        t = proj3[:, :, off:off + D_KV].reshape(b, N_CHUNK, CMP_STRIDE, N_KV_GROUPS, HEAD_DIM)
        return t.transpose(0, 3, 1, 2, 4).reshape(b, N_KV_GROUPS, N_CHUNK, half)

    k_cmp, v_cmp = _compress(
        chunks(OFF_KC), chunks(OFF_VC), pos_cmp_k.reshape(2, half), pos_cmp_v.reshape(2, half),
        w_cmp_k1.reshape(2, half, CMP_HIDDEN), w_cmp_v1.reshape(2, half, CMP_HIDDEN), w_cmp_k2, w_cmp_v2)

    attn = _attention(proj3, gate_logits.reshape(b, s, LANES), k_cmp, v_cmp, _overlap_t(), _block_cols())

    merged = _merge(conv.reshape(m, D_CONV), attn.reshape(m, D_Q), w_conv_out, w_attn_out, gates_ab)
    z = _matmul(merged, w_out, BF16, name="out_proj")
    x1, h = _post_mix(x2, z, norm_mix_post, norm_mlp_pre)

    hidden = _matmul(h, w_up, BF16, name="mlp_up", relu2=True)
    y = _matmul_ktiled(hidden, w_down, BF16, bm=2048, bn=1024, bk=1024, name="mlp_down")
    return _post_mlp(x1, y, norm_mlp_post).reshape(b, s, d)


def kernel(x, norm_mix_pre, w_in, pos_cmp_k, w_cmp_k1, w_cmp_k2, pos_cmp_v, w_cmp_v1, w_cmp_v2, w_dw, b_dw,
           ln_conv_g, ln_conv_b, w_conv_out, w_attn_out, w_out, norm_mix_post, norm_mlp_pre, w_up, w_down,
           norm_mlp_post):
    for l in range(norm_mix_pre.shape[0]):
        x = _layer(x, norm_mix_pre[l], w_in[l], pos_cmp_k[l], w_cmp_k1[l], w_cmp_k2[l], pos_cmp_v[l], w_cmp_v1[l],
                   w_cmp_v2[l], w_dw[l], b_dw[l], ln_conv_g[l], ln_conv_b[l], w_conv_out[l], w_attn_out[l], w_out[l],
                   norm_mix_post[l], norm_mlp_pre[l], w_up[l], w_down[l], norm_mlp_post[l])
    return x
```

```python
import functools

import numpy as np
import jax
import jax.numpy as jnp
from jax import lax
from jax.experimental import pallas as pl
from jax.experimental.pallas import tpu as pltpu

D_MODEL = 4096
SEQ = 2048
D_CONV = D_MODEL // 2
CONV_WIDTH = 31
HEAD_DIM = 128
N_HEADS = 16
N_KV_GROUPS = 4
HEADS_PER_GROUP = N_HEADS // N_KV_GROUPS
CMP_BLOCK = 32
CMP_STRIDE = 16
CMP_HIDDEN = 2 * HEAD_DIM
SLC_BLOCK = 64
N_SELECT = 16
WINDOW = 512
N_NSA_BRANCHES = 3
D_FF = 4 * D_MODEL
D_Q = N_HEADS * HEAD_DIM
D_KV = N_KV_GROUPS * HEAD_DIM
N_GATE = N_NSA_BRANCHES * N_HEADS
D_MAIN = 2 * D_CONV + D_Q + 6 * D_KV
N_CHUNK = SEQ // CMP_STRIDE
N_CMP = N_CHUNK - CMP_BLOCK // CMP_STRIDE + 1
N_SLC = SEQ // SLC_BLOCK

NORM_EPS = 1e-6
LN_EPS = 1e-5
FORCED_SCORE = 1e4
MASKED_SCORE = -1e4
NEG_INF = -1e30
MASK_BIG = 2.0 ** 100
LOG2_E = float(np.log2(np.e))

LANES = 128
SUBLANES = 8
HALO = 32
ATTN_TILE = 256
MM_BM = 2048
MM_BN = 512
VMEM_LIMIT = 56 * 1024 * 1024

OFF_GLU_A = 0
OFF_GLU_B = D_CONV
OFF_Q = 2 * D_CONV
OFF_KC = OFF_Q + D_Q
OFF_VC = OFF_KC + D_KV
OFF_KS = OFF_VC + D_KV
OFF_VS = OFF_KS + D_KV
OFF_KW = OFF_VS + D_KV
OFF_VW = OFF_KW + D_KV

F32 = jnp.float32
BF16 = jnp.bfloat16


def _params(*sem):
    return pltpu.CompilerParams(dimension_semantics=sem, vmem_limit_bytes=VMEM_LIMIT)


def _rmsnorm_gates_kernel(x_ref, g_ref, wt_ref, u_ref, o_ref):
    x = x_ref[...]
    y = x * lax.rsqrt(jnp.mean(x * x, axis=-1, keepdims=True) + NORM_EPS)
    u = (y * g_ref[...]).astype(u_ref.dtype)
    u_ref[...] = u
    o_ref[...] = lax.dot_general(u, wt_ref[...].astype(BF16), (((1,), (1,)), ((), ())), preferred_element_type=F32)


def _rmsnorm_and_branch_gates(x, g, wt, row0, bm=512):
    m, d = x.shape
    return pl.pallas_call(
        _rmsnorm_gates_kernel,
        grid=(m // bm,),
        in_specs=[pl.BlockSpec((bm, d), lambda i: (i, 0)), pl.BlockSpec((1, d), lambda i: (0, 0)),
                  pl.BlockSpec((pl.Element(LANES), pl.Element(d)), lambda i: (row0, 0))],
        out_specs=[pl.BlockSpec((bm, d), lambda i: (i, 0)), pl.BlockSpec((bm, LANES), lambda i: (i, 0))],
        out_shape=[jax.ShapeDtypeStruct((m, d), BF16), jax.ShapeDtypeStruct((m, LANES), F32)],
        compiler_params=_params("parallel"),
        name="rmsnorm_in_branch_gates",
    )(x, g.reshape(1, d), wt)


def _post_mix_kernel(x_ref, z_ref, g1_ref, g2_ref, x1_ref, h_ref):
    z = z_ref[...].astype(F32)
    zn = z * lax.rsqrt(jnp.mean(z * z, axis=-1, keepdims=True) + NORM_EPS)
    x1 = x_ref[...] + zn * g1_ref[...]
    x1_ref[...] = x1
    hn = x1 * lax.rsqrt(jnp.mean(x1 * x1, axis=-1, keepdims=True) + NORM_EPS)
    h_ref[...] = (hn * g2_ref[...]).astype(h_ref.dtype)


def _post_mix(x, z, g1, g2, bm=256):
    m, d = x.shape
    row = pl.BlockSpec((bm, d), lambda i: (i, 0))
    vec = pl.BlockSpec((1, d), lambda i: (0, 0))
    return pl.pallas_call(
        _post_mix_kernel,
        grid=(m // bm,),
        in_specs=[row, row, vec, vec],
        out_specs=[row, row],
        out_shape=[jax.ShapeDtypeStruct((m, d), F32), jax.ShapeDtypeStruct((m, d), BF16)],
        compiler_params=_params("parallel"),
        name="post_mix_norm",
    )(x, z, g1.reshape(1, d), g2.reshape(1, d))


def _post_mlp_kernel(x_ref, y_ref, g_ref, o_ref):
    y = y_ref[...].astype(F32)
    yn = y * lax.rsqrt(jnp.mean(y * y, axis=-1, keepdims=True) + NORM_EPS)
    o_ref[...] = x_ref[...] + yn * g_ref[...]


def _post_mlp(x, y, g, bm=512):
    m, d = x.shape
    row = pl.BlockSpec((bm, d), lambda i: (i, 0))
    return pl.pallas_call(
        _post_mlp_kernel,
        grid=(m // bm,),
        in_specs=[row, row, pl.BlockSpec((1, d), lambda i: (0, 0))],
        out_specs=row,
        out_shape=jax.ShapeDtypeStruct((m, d), F32),
        compiler_params=_params("parallel"),
        name="post_mlp_norm",
    )(x, y, g.reshape(1, d))


_NT_DIMS = (((1,), (1,)), ((), ()))


def _resident_rows_spec(bm, k):
    return pl.BlockSpec((bm, k), lambda i, j: (i, 0), pipeline_mode=pl.Buffered(1))


def _mm_nt_scale_kernel(a_ref, wt_ref, s_ref, o_ref):
    acc = lax.dot_general(a_ref[...], wt_ref[...].astype(BF16), _NT_DIMS, preferred_element_type=F32)
    o_ref[...] = (acc * s_ref[...]).astype(o_ref.dtype)


def _mm_nt_kernel(a_ref, wt_ref, o_ref):
    acc = lax.dot_general(a_ref[...], wt_ref[...].astype(BF16), _NT_DIMS, preferred_element_type=F32)
    o_ref[...] = acc.astype(o_ref.dtype)


def _matmul_nt(a, wt, row0, n, out_dtype, *, name, bm=MM_BM, bn=MM_BN, col_scale=None):
    m, k = a.shape
    in_specs = [_resident_rows_spec(bm, k),
                pl.BlockSpec((pl.Element(bn), pl.Element(k)),
                             lambda i, j: ((row0 // SUBLANES + j * (bn // SUBLANES)) * SUBLANES, 0))]
    args = [a, wt]
    body = _mm_nt_kernel
    if col_scale is not None:
        body = _mm_nt_scale_kernel
        in_specs.append(pl.BlockSpec((1, bn), lambda i, j: (0, j)))
        args.append(col_scale.reshape(1, n))
    return pl.pallas_call(
        body,
        grid=(m // bm, n // bn),
        in_specs=in_specs,
        out_specs=pl.BlockSpec((bm, bn), lambda i, j: (i, j)),
        out_shape=jax.ShapeDtypeStruct((m, n), out_dtype),
        compiler_params=_params("parallel", "arbitrary"),
        name=name,
    )(*args)


def _mm_plain_kernel(a_ref, w_ref, o_ref):
    o_ref[...] = jnp.dot(a_ref[...], w_ref[...].astype(BF16), preferred_element_type=F32).astype(o_ref.dtype)


def _mm_relu2_kernel(a_ref, w_ref, o_ref):
    acc = jnp.dot(a_ref[...], w_ref[...].astype(BF16), preferred_element_type=F32)
    r = jnp.maximum(acc, 0.0)
    o_ref[...] = (r * r).astype(o_ref.dtype)


def _matmul(a, w, out_dtype, *, name, bm=MM_BM, bn=MM_BN, relu2=False):
    m, k = a.shape
    n = w.shape[1]
    return pl.pallas_call(
        _mm_relu2_kernel if relu2 else _mm_plain_kernel,
        grid=(m // bm, n // bn),
        in_specs=[_resident_rows_spec(bm, k), pl.BlockSpec((k, bn), lambda i, j: (0, j))],
        out_specs=pl.BlockSpec((bm, bn), lambda i, j: (i, j)),
        out_shape=jax.ShapeDtypeStruct((m, n), out_dtype),
        compiler_params=_params("parallel", "arbitrary"),
        name=name,
    )(a, w)


def _mm_kacc_kernel(a_ref, w_ref, o_ref, acc_ref):
    kk = pl.program_id(2)

    @pl.when(kk == 0)
    def _():
        acc_ref[...] = jnp.zeros_like(acc_ref)

    acc_ref[...] += jnp.dot(a_ref[...], w_ref[...].astype(BF16), preferred_element_type=F32)

    @pl.when(kk == pl.num_programs(2) - 1)
    def _():
        o_ref[...] = acc_ref[...].astype(o_ref.dtype)


def _matmul_ktiled(a, w, out_dtype, *, bk, name, bm=MM_BM, bn=MM_BN):
    m, k = a.shape
    n = w.shape[1]
    return pl.pallas_call(
        _mm_kacc_kernel,
        grid=(m // bm, n // bn, k // bk),
        in_specs=[pl.BlockSpec((bm, bk), lambda i, j, kk: (i, kk)), pl.BlockSpec((bk, bn), lambda i, j, kk: (kk, j))],
        out_specs=pl.BlockSpec((bm, bn), lambda i, j, kk: (i, j)),
        out_shape=jax.ShapeDtypeStruct((m, n), out_dtype),
        scratch_shapes=[pltpu.VMEM((bm, bn), F32)],
        compiler_params=_params("parallel", "parallel", "arbitrary"),
        name=name,
    )(a, w)


def _merge_kernel(c_ref, a_ref, wc_ref, wa_ref, ga_ref, gb_ref, o_ref):
    yc = jnp.dot(c_ref[...], wc_ref[...].astype(BF16), preferred_element_type=F32)
    ya = jnp.dot(a_ref[...], wa_ref[...].astype(BF16), preferred_element_type=F32)
    ga = jax.nn.sigmoid(ga_ref[...].astype(F32))
    gb = jax.nn.sigmoid(gb_ref[...].astype(F32))
    o_ref[...] = (ga * yc + gb * ya).astype(o_ref.dtype)


def _merge(conv_act, attn_act, wc, wa, gates_ab, *, bm=MM_BM // 2, bn=MM_BN):
    m, kc = conv_act.shape
    ka = attn_act.shape[1]
    n = wc.shape[1]
    jb = n // bn
    return pl.pallas_call(
        _merge_kernel,
        grid=(m // bm, n // bn),
        in_specs=[
            pl.BlockSpec((bm, kc), lambda i, j: (i, 0)),
            pl.BlockSpec((bm, ka), lambda i, j: (i, 0)),
            pl.BlockSpec((kc, bn), lambda i, j: (0, j)),
            pl.BlockSpec((ka, bn), lambda i, j: (0, j)),
            pl.BlockSpec((bm, bn), lambda i, j: (i, j)),
            pl.BlockSpec((bm, bn), lambda i, j: (i, jb + j)),
        ],
        out_specs=pl.BlockSpec((bm, bn), lambda i, j: (i, j)),
        out_shape=jax.ShapeDtypeStruct((m, n), BF16),
        compiler_params=_params("parallel", "arbitrary"),
        name="gated_merge",
    )(conv_act, attn_act, wc, wa, gates_ab, gates_ab)


def _gelu_tanh(x):
    return 0.5 * x * (1.0 + jnp.tanh(np.sqrt(2.0 / np.pi) * (x + 0.044715 * (x * x * x))))


def _compress_kernel(xk_ref, xv_ref, pk_ref, pv_ref, w1k_ref, w1v_ref, w2k_ref, w2v_ref, ok_ref, ov_ref):
    for x_ref, p_ref, w1_ref, w2_ref, o_ref in ((xk_ref, pk_ref, w1k_ref, w2k_ref, ok_ref),
                                                (xv_ref, pv_ref, w1v_ref, w2v_ref, ov_ref)):
        x = x_ref[0, 0].astype(F32)
        first = jnp.dot((x + p_ref[0:1, :]).astype(BF16), w1_ref[0].astype(BF16), preferred_element_type=F32)
        second = jnp.dot((x + p_ref[1:2, :]).astype(BF16), w1_ref[1].astype(BF16), preferred_element_type=F32)
        h = first + pltpu.roll(second, N_CHUNK - 1, axis=0)
        o_ref[0, 0] = jnp.dot(_gelu_tanh(h).astype(BF16), w2_ref[...].astype(BF16),
                              preferred_element_type=F32).astype(o_ref.dtype)


def _compress(xk, xv, pk, pv, w1k, w1v, w2k, w2v):
    b = xk.shape[0]
    half = CMP_STRIDE * HEAD_DIM
    xspec = pl.BlockSpec((1, 1, N_CHUNK, half), lambda i, g: (i, g, 0, 0))
    pspec = pl.BlockSpec((2, half), lambda i, g: (0, 0))
    w1spec = pl.BlockSpec((2, half, CMP_HIDDEN), lambda i, g: (0, 0, 0))
    w2spec = pl.BlockSpec((CMP_HIDDEN, HEAD_DIM), lambda i, g: (0, 0))
    ospec = pl.BlockSpec((1, 1, N_CHUNK, HEAD_DIM), lambda i, g: (i, g, 0, 0))
    oshape = jax.ShapeDtypeStruct((b, N_KV_GROUPS, N_CHUNK, HEAD_DIM), BF16)
    return pl.pallas_call(
        _compress_kernel,
        grid=(b, N_KV_GROUPS),
        in_specs=[xspec, xspec, pspec, pspec, w1spec, w1spec, w2spec, w2spec],
        out_specs=[ospec, ospec],
        out_shape=[oshape, oshape],
        compiler_params=_params("parallel", "parallel"),
        name="compress_kv",
    )(xk, xv, pk, pv, w1k, w1v, w2k, w2v)


def _attn_kernel(q_ref, ks_ref, vs_ref, kw_ref, vw_ref, kc_ref, vc_ref, gl_ref, ovl_ref, blk_ref, o_ref,
                 s_ref, mx_ref, acc_ref, sw_ref, ow_ref, *, tq):
    tk = tq
    grp = pl.program_id(1)
    qi = pl.program_id(2)
    q0 = qi * tq
    nh = HEADS_PER_GROUP
    rows = nh * tq
    nt = (((1,), (1,)), ((), ()))

    q = q_ref[0]
    qs = jnp.concatenate([q[:, r * HEAD_DIM:(r + 1) * HEAD_DIM] for r in range(nh)], axis=0)

    n_win = WINDOW // tk + 1
    wk0 = pl.multiple_of(jnp.maximum(qi - (n_win - 1), 0) * tk, tk)
    sc = lax.dot_general(qs, kw_ref[0, pl.ds(wk0, n_win * tk), :], nt, preferred_element_type=F32)
    key = wk0 + lax.broadcasted_iota(jnp.int32, (tq, n_win * tk), 1)
    t_abs = q0 + lax.broadcasted_iota(jnp.int32, (tq, n_win * tk), 0)
    win_bias = jnp.where((key <= t_abs) & (key > t_abs - WINDOW), 0.0, NEG_INF)
    m_win = None
    for i in range(n_win):
        part = sc[:, i * tk:(i + 1) * tk].reshape(nh, tq, tk) + win_bias[None, :, i * tk:(i + 1) * tk]
        part = part.reshape(rows, tk)
        sw_ref[i] = part
        for c in range(tk // LANES):
            piece = part[:, c * LANES:(c + 1) * LANES]
            m_win = piece if m_win is None else jnp.maximum(m_win, piece)
    n_pad = jnp.maximum(WINDOW - 1 - (q0 + lax.broadcasted_iota(jnp.int32, (tq, LANES), 0)), 0).astype(F32)
    n_pad = jnp.concatenate([n_pad] * nh, axis=0)
    m_win = jnp.broadcast_to(jnp.max(m_win, axis=-1, keepdims=True), (rows, LANES))
    m_win = jnp.where(n_pad > 0.0, jnp.maximum(m_win, 0.0), m_win)
    m_wide = jnp.concatenate([m_win] * (tk // LANES), axis=1)
    pr = jnp.concatenate([jnp.exp2(sw_ref[i] - m_wide).astype(BF16) for i in range(n_win)], axis=1)
    v_ext = jnp.concatenate([vw_ref[0, pl.ds(wk0, n_win * tk), :], jnp.ones((n_win * tk, LANES), BF16)], axis=1)
    acc_w = jnp.dot(pr, v_ext, preferred_element_type=F32)
    pad_term = n_pad * jnp.exp2(jnp.where(n_pad > 0.0, -m_win, 0.0))
    ow_ref[...] = acc_w[:, :HEAD_DIM] * (1.0 / (acc_w[:, HEAD_DIM:] + pad_term))

    s = lax.dot_general(qs, kc_ref[0, 0], nt, preferred_element_type=F32).reshape(nh, tq, N_CHUNK)
    n_idx = lax.broadcasted_iota(jnp.int32, (tq, N_CHUNK), 1)
    t_idx = q0 + lax.broadcasted_iota(jnp.int32, (tq, N_CHUNK), 0)
    vis = ((n_idx * CMP_STRIDE + (CMP_BLOCK - 1) <= t_idx) & (n_idx < N_CMP))[None]
    sm = jnp.where(vis, s, NEG_INF)
    mx = jnp.maximum(jnp.max(sm, axis=-1, keepdims=True), 0.5 * NEG_INF)
    e = jnp.exp2(sm - mx)
    p = e * (1.0 / jnp.maximum(jnp.sum(e, axis=-1, keepdims=True), 1e-30))
    o_cmp = jnp.dot(p.reshape(rows, N_CHUNK).astype(BF16), vc_ref[0, 0], preferred_element_type=F32)

    p_sum = p[0] + p[1] + p[2] + p[3]
    p_hi = p_sum.astype(BF16)
    rem = p_sum - p_hi.astype(F32)
    p_mid = rem.astype(BF16)
    p_lo = (rem - p_mid.astype(F32)).astype(BF16)
    ovl = ovl_ref[...]
    imp = (lax.dot_general(ovl, p_hi, nt, preferred_element_type=F32)
           + lax.dot_general(ovl, p_mid, nt, preferred_element_type=F32)
           + lax.dot_general(ovl, p_lo, nt, preferred_element_type=F32))

    j_idx = lax.broadcasted_iota(jnp.int32, (N_SLC, tq), 0)
    cur = jnp.right_shift(q0 + lax.broadcasted_iota(jnp.int32, (N_SLC, tq), 1), int(np.log2(SLC_BLOCK)))
    valid = j_idx <= cur
    forced = (j_idx == 0) | (j_idx == cur) | (j_idx == cur - 1)
    score = jnp.where(forced & valid, FORCED_SCORE, jnp.where(valid, imp, MASKED_SCORE))
    rank = jnp.zeros((N_SLC, tq), jnp.int32)
    for k in range(N_SLC):
        sk = score[k:k + 1, :]
        beats = (sk > score) | ((sk == score) & (j_idx > k))
        rank = rank + beats.astype(jnp.int32)
    unsel_t = jnp.where(rank < N_SELECT, 0.0, 1.0)
    unsel_t = jnp.concatenate([unsel_t, jnp.zeros((LANES - N_SLC, tq), F32)], axis=0)
    unsel = jnp.transpose(unsel_t).astype(BF16)

    q_slc = jnp.concatenate([qs, jnp.concatenate([unsel] * nh, axis=0)], axis=1)

    local_q = lax.broadcasted_iota(jnp.int32, (tq, tk), 0)
    local_k = lax.broadcasted_iota(jnp.int32, (tq, tk), 1)
    causal_bias = jnp.where(local_k <= local_q, 0.0, NEG_INF)

    def pass1(qa, k_ref, with_blocks, kt0, slot0, biases, init):
        n = len(biases)
        k0 = pl.multiple_of(kt0 * tk, tk)
        k = k_ref[0, pl.ds(k0, n * tk), :]
        if with_blocks:
            k = jnp.concatenate([k, blk_ref[pl.ds(k0, n * tk), :]], axis=1)
        sc = lax.dot_general(qa, k, nt, preferred_element_type=F32)
        m = None
        for i, bias in enumerate(biases):
            part = sc[:, i * tk:(i + 1) * tk]
            if bias is not None:
                part = (part.reshape(nh, tq, tk) + bias[None]).reshape(rows, tk)
            s_ref[slot0 + i] = part
            for c in range(tk // LANES):
                piece = part[:, c * LANES:(c + 1) * LANES]
                m = piece if m is None else jnp.maximum(m, piece)
        mx_ref[...] = m if init else jnp.maximum(mx_ref[...], m)

    def pass2(v_ref, kt0, slot0, n, init):
        k0 = pl.multiple_of(kt0 * tk, tk)
        v_ext = jnp.concatenate([v_ref[0, pl.ds(k0, n * tk), :], jnp.ones((n * tk, LANES), BF16)], axis=1)
        m_rep = mx_ref[...]
        m_wide = jnp.concatenate([m_rep] * (tk // LANES), axis=1)
        pr = jnp.concatenate([jnp.exp2(s_ref[slot0 + i] - m_wide).astype(BF16) for i in range(n)], axis=1)
        d = jnp.dot(pr, v_ext, preferred_element_type=F32)
        if init:
            acc_ref[...] = d
        else:
            acc_ref[...] += d

    def spread_row_max():
        mx_ref[...] = jnp.broadcast_to(jnp.max(mx_ref[...], axis=-1, keepdims=True), (rows, LANES))

    pairs = jnp.right_shift(qi, 1)
    odd = jnp.bitwise_and(qi, 1) == 1

    pass1(q_slc, ks_ref, True, qi, qi, [causal_bias], True)

    def slc_pass1(i, carry):
        pass1(q_slc, ks_ref, True, 2 * i, 2 * i, [None, None], False)
        return carry

    lax.fori_loop(0, pairs, slc_pass1, 0)

    @pl.when(odd)
    def _():
        pass1(q_slc, ks_ref, True, qi - 1, qi - 1, [None], False)

    spread_row_max()
    pass2(vs_ref, qi, qi, 1, True)

    def slc_pass2(i, carry):
        pass2(vs_ref, 2 * i, 2 * i, 2, False)
        return carry

    lax.fori_loop(0, pairs, slc_pass2, 0)

    @pl.when(odd)
    def _():
        pass2(vs_ref, qi - 1, qi - 1, 1, False)

    acc = acc_ref[...]
    o_slc = acc[:, :HEAD_DIM] * (1.0 / acc[:, HEAD_DIM:])

    o_win = ow_ref[...]

    gates = jax.nn.sigmoid(gl_ref[0])
    gates = pltpu.roll(gates, (grp * (LANES - nh * N_NSA_BRANCHES)) % LANES, axis=1)
    for r in range(nh):
        sl = slice(r * tq, (r + 1) * tq)
        c = r * N_NSA_BRANCHES
        o_r = (gates[:, c:c + 1] * o_cmp[sl] + gates[:, c + 1:c + 2] * o_slc[sl] + gates[:, c + 2:c + 3] * o_win[sl])
        o_ref[0, :, r * HEAD_DIM:(r + 1) * HEAD_DIM] = o_r.astype(o_ref.dtype)


def _attention(proj3, gate_logits3, k_cmp, v_cmp, overlap_t, block_cols, *, tq=ATTN_TILE):
    b = proj3.shape[0]
    gw = HEADS_PER_GROUP * HEAD_DIM
    kv = lambda off: pl.BlockSpec((1, SEQ, HEAD_DIM), lambda i, g, t: (i, 0, off // HEAD_DIM + g))
    cmp_spec = pl.BlockSpec((1, 1, N_CHUNK, HEAD_DIM), lambda i, g, t: (i, g, 0, 0))
    rows = HEADS_PER_GROUP * tq
    return pl.pallas_call(
        functools.partial(_attn_kernel, tq=tq),
        grid=(b, N_KV_GROUPS, SEQ // tq),
        in_specs=[
            pl.BlockSpec((1, tq, gw), lambda i, g, t: (i, t, OFF_Q // gw + g)),
            kv(OFF_KS), kv(OFF_VS), kv(OFF_KW), kv(OFF_VW),
            cmp_spec, cmp_spec,
            pl.BlockSpec((1, tq, LANES), lambda i, g, t: (i, t, 0)),
            pl.BlockSpec((N_SLC, N_CHUNK), lambda i, g, t: (0, 0)),
            pl.BlockSpec((SEQ, LANES), lambda i, g, t: (0, 0)),
        ],
        out_specs=pl.BlockSpec((1, tq, gw), lambda i, g, t: (i, t, g)),
        out_shape=jax.ShapeDtypeStruct((b, SEQ, D_Q), BF16),
        scratch_shapes=[
            pltpu.VMEM((SEQ // tq, rows, tq), F32),
            pltpu.VMEM((rows, LANES), F32),
            pltpu.VMEM((rows, 2 * HEAD_DIM), F32),
            pltpu.VMEM((WINDOW // tq + 1, rows, tq), F32),
            pltpu.VMEM((rows, HEAD_DIM), F32),
        ],
        compiler_params=_params("parallel", "parallel", "arbitrary"),
        name="nsa_attention",
    )(proj3, proj3, proj3, proj3, proj3, k_cmp, v_cmp, gate_logits3, overlap_t, block_cols)


def _conv_kernel(a_ref, b_ref, ah_ref, bh_ref, w_ref, bias_ref, lg_ref, lb_ref, o_ref, u_ref, y_ref, *, ts):
    nc = D_CONV // LANES
    first = pl.program_id(1) == 0
    u_main = a_ref[0].astype(F32) * jax.nn.sigmoid(b_ref[0].astype(F32))
    u_halo = ah_ref[0].astype(F32) * jax.nn.sigmoid(bh_ref[0].astype(F32))
    u_halo = jnp.where(first, 0.0, u_halo)
    for c in range(nc):
        u_ref[c, 0:HALO, :] = u_halo[:, c * LANES:(c + 1) * LANES]
        u_ref[c, HALO:HALO + ts, :] = u_main[:, c * LANES:(c + 1) * LANES]

    rc = 64
    base = HALO - (CONV_WIDTH - 1)

    def chunk_body(c, carry):
        for r0 in range(0, ts, rc):
            acc = jnp.zeros((rc, LANES), F32)
            for j in range(CONV_WIDTH):
                acc = acc + u_ref[c, pl.ds(base + r0 + j, rc), :] * w_ref[c, j:j + 1, :]
            y_ref[c, r0:r0 + rc, :] = acc
        return carry

    lax.fori_loop(0, nc, chunk_body, 0)

    y = jnp.concatenate([y_ref[c] for c in range(nc)], axis=1) + bias_ref[...]
    mu = jnp.mean(y, axis=-1, keepdims=True)
    d = y - mu
    var = jnp.mean(d * d, axis=-1, keepdims=True)
    z = d * lax.rsqrt(var + LN_EPS) * lg_ref[...] + lb_ref[...]
    o_ref[0] = (z * jax.nn.sigmoid(z)).astype(o_ref.dtype)


def _conformer_conv(proj3, w_chunks, b_dw, ln_g, ln_b, *, ts=256):
    b = proj3.shape[0]
    nc = D_CONV // LANES
    per = ts // HALO
    main = lambda off: pl.BlockSpec((1, ts, D_CONV), lambda i, t: (i, t, off // D_CONV))
    halo = lambda off: pl.BlockSpec((1, HALO, D_CONV), lambda i, t: (i, jnp.maximum(t * per - 1, 0), off // D_CONV))
    vec = pl.BlockSpec((1, D_CONV), lambda i, t: (0, 0))
    return pl.pallas_call(
        functools.partial(_conv_kernel, ts=ts),
        grid=(b, SEQ // ts),
        in_specs=[main(OFF_GLU_A), main(OFF_GLU_B), halo(OFF_GLU_A), halo(OFF_GLU_B),
                  pl.BlockSpec((nc, HALO, LANES), lambda i, t: (0, 0, 0)), vec, vec, vec],
        out_specs=pl.BlockSpec((1, ts, D_CONV), lambda i, t: (i, t, 0)),
        out_shape=jax.ShapeDtypeStruct((b, SEQ, D_CONV), BF16),
        scratch_shapes=[pltpu.VMEM((nc, HALO + ts, LANES), F32), pltpu.VMEM((nc, ts, LANES), F32)],
        compiler_params=_params("parallel", "arbitrary"),
        name="conformer_conv",
    )(proj3, proj3, proj3, proj3, w_chunks, b_dw.reshape(1, D_CONV), ln_g.reshape(1, D_CONV), ln_b.reshape(1, D_CONV))


def _overlap_t():
    cmp_start = np.arange(N_CHUNK) * CMP_STRIDE
    slc_start = np.arange(N_SLC) * SLC_BLOCK
    ov = ((cmp_start[None, :] < slc_start[:, None] + SLC_BLOCK) & (cmp_start[None, :] + CMP_BLOCK > slc_start[:, None])
          & (np.arange(N_CHUNK)[None, :] < N_CMP))
    return jnp.asarray(ov.astype(np.float32), dtype=BF16)


def _block_cols():
    own = (np.arange(SEQ)[:, None] // SLC_BLOCK) == np.arange(LANES)[None, :]
    return jnp.asarray(np.where(own, -MASK_BIG, 0.0).astype(np.float32), dtype=BF16)


def _layer(x, norm_mix_pre, w_in, pos_cmp_k, w_cmp_k1, w_cmp_k2, pos_cmp_v, w_cmp_v1, w_cmp_v2,
           w_dw, b_dw, ln_conv_g, ln_conv_b, w_conv_out, w_attn_out, w_out, norm_mix_post,
           norm_mlp_pre, w_up, w_down, norm_mlp_post):
    b, s, d = x.shape
    m = b * s
    x2 = x.reshape(m, d)
    half = CMP_STRIDE * HEAD_DIM

    w_in_t = jnp.swapaxes(w_in, 0, 1)
    col_scale = jnp.ones((D_MAIN,), F32).at[OFF_Q:OFF_Q + D_Q].set(HEAD_DIM ** -0.5 * LOG2_E)

    u, gate_logits = _rmsnorm_and_branch_gates(x2, norm_mix_pre, w_in_t, D_MAIN)
    proj = _matmul_nt(u, w_in_t, 0, D_MAIN, BF16, name="in_proj", col_scale=col_scale)
    proj3 = proj.reshape(b, s, D_MAIN)

    w_chunks = jnp.pad(w_dw.reshape(CONV_WIDTH, D_CONV), ((0, HALO - CONV_WIDTH), (0, 0)))
    w_chunks = w_chunks.reshape(HALO, D_CONV // LANES, LANES).transpose(1, 0, 2)
    gates_ab = _matmul_nt(u, w_in_t, D_MAIN + N_GATE, 2 * D_MODEL, BF16, name="in_proj_merge_gates")
    conv = _conformer_conv(proj3, w_chunks, b_dw, ln_conv_g, ln_conv_b)

    def chunks(off):
        t = proj3[:, :, off:off + D_KV].reshape(b, N_CHUNK, CMP_STRIDE, N_KV_GROUPS, HEAD_DIM)
        return t.transpose(0, 3, 1, 2, 4).reshape(b, N_KV_GROUPS, N_CHUNK, half)

    k_cmp, v_cmp = _compress(
        chunks(OFF_KC), chunks(OFF_VC), pos_cmp_k.reshape(2, half), pos_cmp_v.reshape(2, half),
        w_cmp_k1.reshape(2, half, CMP_HIDDEN), w_cmp_v1.reshape(2, half, CMP_HIDDEN), w_cmp_k2, w_cmp_v2)

    attn = _attention(proj3, gate_logits.reshape(b, s, LANES), k_cmp, v_cmp, _overlap_t(), _block_cols())

    merged = _merge(conv.reshape(m, D_CONV), attn.reshape(m, D_Q), w_conv_out, w_attn_out, gates_ab)
    z = _matmul(merged, w_out, BF16, name="out_proj")
    x1, h = _post_mix(x2, z, norm_mix_post, norm_mlp_pre)

    hidden = _matmul(h, w_up, BF16, name="mlp_up", relu2=True)
    y = _matmul_ktiled(hidden, w_down, BF16, bm=2048, bn=1024, bk=1024, name="mlp_down")
    return _post_mlp(x1, y, norm_mlp_post).reshape(b, s, d)


def kernel(x, norm_mix_pre, w_in, pos_cmp_k, w_cmp_k1, w_cmp_k2, pos_cmp_v, w_cmp_v1, w_cmp_v2, w_dw, b_dw,
           ln_conv_g, ln_conv_b, w_conv_out, w_attn_out, w_out, norm_mix_post, norm_mlp_pre, w_up, w_down,
           norm_mlp_post):
    for l in range(norm_mix_pre.shape[0]):
        x = _layer(x, norm_mix_pre[l], w_in[l], pos_cmp_k[l], w_cmp_k1[l], w_cmp_k2[l], pos_cmp_v[l], w_cmp_v1[l],
                   w_cmp_v2[l], w_dw[l], b_dw[l], ln_conv_g[l], ln_conv_b[l], w_conv_out[l], w_attn_out[l], w_out[l],
                   norm_mix_post[l], norm_mlp_pre[l], w_up[l], w_down[l], norm_mlp_post[l])
    return x
```

```python
import functools

import numpy as np
import jax
import jax.numpy as jnp
from jax import lax
from jax.experimental import pallas as pl
from jax.experimental.pallas import tpu as pltpu

D_MODEL = 4096
SEQ = 2048
D_CONV = D_MODEL // 2
CONV_WIDTH = 31
HEAD_DIM = 128
N_HEADS = 16
N_KV_GROUPS = 4
HEADS_PER_GROUP = N_HEADS // N_KV_GROUPS
CMP_BLOCK = 32
CMP_STRIDE = 16
CMP_HIDDEN = 2 * HEAD_DIM
SLC_BLOCK = 64
N_SELECT = 16
WINDOW = 512
N_NSA_BRANCHES = 3
D_Q = N_HEADS * HEAD_DIM
D_KV = N_KV_GROUPS * HEAD_DIM
N_GATE = N_NSA_BRANCHES * N_HEADS
D_MAIN = 2 * D_CONV + D_Q + 6 * D_KV
N_CHUNK = SEQ // CMP_STRIDE
N_CMP = N_CHUNK - CMP_BLOCK // CMP_STRIDE + 1
N_SLC = SEQ // SLC_BLOCK

NORM_EPS = 1e-6
LN_EPS = 1e-5
FORCED_SCORE = 1e4
MASKED_SCORE = -1e4
NEG_INF = -1e30
MASK_BIG = 2.0 ** 100
LOG2_E = float(np.log2(np.e))

LANES = 128
SUBLANES = 8
HALO = 32
ATTN_TILE = 256
MM_BM = 2048
MM_BN = 512
VMEM_LIMIT = 56 * 1024 * 1024

OFF_GLU_A = 0
OFF_GLU_B = D_CONV
OFF_Q = 2 * D_CONV
OFF_KC = OFF_Q + D_Q
OFF_VC = OFF_KC + D_KV
OFF_KS = OFF_VC + D_KV
OFF_VS = OFF_KS + D_KV
OFF_KW = OFF_VS + D_KV
OFF_VW = OFF_KW + D_KV

F32 = jnp.float32
BF16 = jnp.bfloat16


def _params(*sem):
    return pltpu.CompilerParams(dimension_semantics=sem, vmem_limit_bytes=VMEM_LIMIT)


def _rmsnorm_gates_kernel(x_ref, g_ref, wt_ref, u_ref, o_ref):
    x = x_ref[...]
    y = x * lax.rsqrt(jnp.mean(x * x, axis=-1, keepdims=True) + NORM_EPS)
    u = (y * g_ref[...]).astype(u_ref.dtype)
    u_ref[...] = u
    o_ref[...] = lax.dot_general(u, wt_ref[...].astype(BF16), (((1,), (1,)), ((), ())), preferred_element_type=F32)


def _rmsnorm_and_branch_gates(x, g, wt, row0, bm=512):
    m, d = x.shape
    return pl.pallas_call(
        _rmsnorm_gates_kernel,
        grid=(m // bm,),
        in_specs=[pl.BlockSpec((bm, d), lambda i: (i, 0)), pl.BlockSpec((1, d), lambda i: (0, 0)),
                  pl.BlockSpec((pl.Element(LANES), pl.Element(d)), lambda i: (row0, 0))],
        out_specs=[pl.BlockSpec((bm, d), lambda i: (i, 0)), pl.BlockSpec((bm, LANES), lambda i: (i, 0))],
        out_shape=[jax.ShapeDtypeStruct((m, d), BF16), jax.ShapeDtypeStruct((m, LANES), F32)],
        compiler_params=_params("parallel"),
        name="rmsnorm_in_branch_gates",
    )(x, g.reshape(1, d), wt)


def _post_mix_kernel(x_ref, z_ref, g1_ref, g2_ref, x1_ref, h_ref):
    z = z_ref[...].astype(F32)
    zn = z * lax.rsqrt(jnp.mean(z * z, axis=-1, keepdims=True) + NORM_EPS)
    x1 = x_ref[...] + zn * g1_ref[...]
    x1_ref[...] = x1
    hn = x1 * lax.rsqrt(jnp.mean(x1 * x1, axis=-1, keepdims=True) + NORM_EPS)
    h_ref[...] = (hn * g2_ref[...]).astype(h_ref.dtype)


def _post_mix(x, z, g1, g2, bm=256):
    m, d = x.shape
    row = pl.BlockSpec((bm, d), lambda i: (i, 0))
    vec = pl.BlockSpec((1, d), lambda i: (0, 0))
    return pl.pallas_call(
        _post_mix_kernel,
        grid=(m // bm,),
        in_specs=[row, row, vec, vec],
        out_specs=[row, row],
        out_shape=[jax.ShapeDtypeStruct((m, d), F32), jax.ShapeDtypeStruct((m, d), BF16)],
        compiler_params=_params("parallel"),
        name="post_mix_norm",
    )(x, z, g1.reshape(1, d), g2.reshape(1, d))


def _post_mlp_kernel(x_ref, y_ref, g_ref, o_ref):
    y = y_ref[...].astype(F32)
    yn = y * lax.rsqrt(jnp.mean(y * y, axis=-1, keepdims=True) + NORM_EPS)
    o_ref[...] = x_ref[...] + yn * g_ref[...]


def _post_mlp(x, y, g, bm=512):
    m, d = x.shape
    row = pl.BlockSpec((bm, d), lambda i: (i, 0))
    return pl.pallas_call(
        _post_mlp_kernel,
        grid=(m // bm,),
        in_specs=[row, row, pl.BlockSpec((1, d), lambda i: (0, 0))],
        out_specs=row,
        out_shape=jax.ShapeDtypeStruct((m, d), F32),
        compiler_params=_params("parallel"),
        name="post_mlp_norm",
    )(x, y, g.reshape(1, d))


_NT_DIMS = (((1,), (1,)), ((), ()))


def _resident_rows_spec(bm, k):
    return pl.BlockSpec((bm, k), lambda i, j: (i, 0), pipeline_mode=pl.Buffered(1))


def _mm_nt_scale_kernel(a_ref, wt_ref, s_ref, o_ref):
    acc = lax.dot_general(a_ref[...], wt_ref[...].astype(BF16), _NT_DIMS, preferred_element_type=F32)
    o_ref[...] = (acc * s_ref[...]).astype(o_ref.dtype)


def _mm_nt_kernel(a_ref, wt_ref, o_ref):
    acc = lax.dot_general(a_ref[...], wt_ref[...].astype(BF16), _NT_DIMS, preferred_element_type=F32)
    o_ref[...] = acc.astype(o_ref.dtype)


def _matmul_nt(a, wt, row0, n, out_dtype, *, name, bm=MM_BM, bn=MM_BN, col_scale=None):
    m, k = a.shape
    in_specs = [_resident_rows_spec(bm, k),
                pl.BlockSpec((pl.Element(bn), pl.Element(k)),
                             lambda i, j: ((row0 // SUBLANES + j * (bn // SUBLANES)) * SUBLANES, 0))]
    args = [a, wt]
    body = _mm_nt_kernel
    if col_scale is not None:
        body = _mm_nt_scale_kernel
        in_specs.append(pl.BlockSpec((1, bn), lambda i, j: (0, j)))
        args.append(col_scale.reshape(1, n))
    return pl.pallas_call(
        body,
        grid=(m // bm, n // bn),
        in_specs=in_specs,
        out_specs=pl.BlockSpec((bm, bn), lambda i, j: (i, j)),
        out_shape=jax.ShapeDtypeStruct((m, n), out_dtype),
        compiler_params=_params("parallel", "arbitrary"),
        name=name,
    )(*args)


def _mm_plain_kernel(a_ref, w_ref, o_ref):
    o_ref[...] = jnp.dot(a_ref[...], w_ref[...].astype(BF16), preferred_element_type=F32).astype(o_ref.dtype)


def _mm_relu2_kernel(a_ref, w_ref, o_ref):
    acc = jnp.dot(a_ref[...], w_ref[...].astype(BF16), preferred_element_type=F32)
    r = jnp.maximum(acc, 0.0)
    o_ref[...] = (r * r).astype(o_ref.dtype)


def _matmul(a, w, out_dtype, *, name, bm=MM_BM, bn=MM_BN, relu2=False):
    m, k = a.shape
    n = w.shape[1]
    return pl.pallas_call(
        _mm_relu2_kernel if relu2 else _mm_plain_kernel,
        grid=(m // bm, n // bn),
        in_specs=[_resident_rows_spec(bm, k), pl.BlockSpec((k, bn), lambda i, j: (0, j))],
        out_specs=pl.BlockSpec((bm, bn), lambda i, j: (i, j)),
        out_shape=jax.ShapeDtypeStruct((m, n), out_dtype),
        compiler_params=_params("parallel", "arbitrary"),
        name=name,
    )(a, w)


def _mm_kacc_kernel(a_ref, w_ref, o_ref, acc_ref):
    kk = pl.program_id(2)

    @pl.when(kk == 0)
    def _():
        acc_ref[...] = jnp.zeros_like(acc_ref)

    acc_ref[...] += jnp.dot(a_ref[...], w_ref[...].astype(BF16), preferred_element_type=F32)

    @pl.when(kk == pl.num_programs(2) - 1)
    def _():
        o_ref[...] = acc_ref[...].astype(o_ref.dtype)


def _matmul_ktiled(a, w, out_dtype, *, bk, name, bm=MM_BM, bn=MM_BN):
    m, k = a.shape
    n = w.shape[1]
    return pl.pallas_call(
        _mm_kacc_kernel,
        grid=(m // bm, n // bn, k // bk),
        in_specs=[pl.BlockSpec((bm, bk), lambda i, j, kk: (i, kk)), pl.BlockSpec((bk, bn), lambda i, j, kk: (kk, j))],
        out_specs=pl.BlockSpec((bm, bn), lambda i, j, kk: (i, j)),
        out_shape=jax.ShapeDtypeStruct((m, n), out_dtype),
        scratch_shapes=[pltpu.VMEM((bm, bn), F32)],
        compiler_params=_params("parallel", "parallel", "arbitrary"),
        name=name,
    )(a, w)


def _merge_kernel(c_ref, a_ref, wc_ref, wa_ref, ga_ref, gb_ref, o_ref):
    yc = jnp.dot(c_ref[...], wc_ref[...].astype(BF16), preferred_element_type=F32)
    ya = jnp.dot(a_ref[...], wa_ref[...].astype(BF16), preferred_element_type=F32)
    ga = jax.nn.sigmoid(ga_ref[...].astype(F32))
    gb = jax.nn.sigmoid(gb_ref[...].astype(F32))
    o_ref[...] = (ga * yc + gb * ya).astype(o_ref.dtype)


def _merge(conv_act, attn_act, wc, wa, gates_ab, *, bm=MM_BM // 2, bn=MM_BN):
    m, kc = conv_act.shape
    ka = attn_act.shape[1]
    n = wc.shape[1]
    jb = n // bn
    return pl.pallas_call(
        _merge_kernel,
        grid=(m // bm, n // bn),
        in_specs=[
            pl.BlockSpec((bm, kc), lambda i, j: (i, 0)),
            pl.BlockSpec((bm, ka), lambda i, j: (i, 0)),
            pl.BlockSpec((kc, bn), lambda i, j: (0, j)),
            pl.BlockSpec((ka, bn), lambda i, j: (0, j)),
            pl.BlockSpec((bm, bn), lambda i, j: (i, j)),
            pl.BlockSpec((bm, bn), lambda i, j: (i, jb + j)),
        ],
        out_specs=pl.BlockSpec((bm, bn), lambda i, j: (i, j)),
        out_shape=jax.ShapeDtypeStruct((m, n), BF16),
        compiler_params=_params("parallel", "arbitrary"),
        name="gated_merge",
    )(conv_act, attn_act, wc, wa, gates_ab, gates_ab)


def _gelu_tanh(x):
    return 0.5 * x * (1.0 + jnp.tanh(np.sqrt(2.0 / np.pi) * (x + 0.044715 * (x * x * x))))


def _compress_kernel(xk_ref, xv_ref, pk_ref, pv_ref, w1k_ref, w1v_ref, w2k_ref, w2v_ref, ok_ref, ov_ref,
                     pair_ref, flat_ref):
    for x_ref, p_ref, w1_ref, w2_ref, o_ref in ((xk_ref, pk_ref, w1k_ref, w2k_ref, ok_ref),
                                                (xv_ref, pv_ref, w1v_ref, w2v_ref, ov_ref)):
        pair_ref[...] = pltpu.bitcast(x_ref[0], jnp.uint32)
        for l in range(CMP_STRIDE):
            word = pair_ref[pl.ds(l // 2, N_CHUNK, stride=CMP_STRIDE // 2), :]
            bits = (word << 16) if l % 2 == 0 else (word & jnp.uint32(0xFFFF0000))
            flat_ref[:, l * HEAD_DIM:(l + 1) * HEAD_DIM] = pltpu.bitcast(bits, F32)
        x = flat_ref[...]
        first = jnp.dot((x + p_ref[0:1, :]).astype(BF16), w1_ref[0].astype(BF16), preferred_element_type=F32)
        second = jnp.dot((x + p_ref[1:2, :]).astype(BF16), w1_ref[1].astype(BF16), preferred_element_type=F32)
        h = first + pltpu.roll(second, N_CHUNK - 1, axis=0)
        o_ref[0, 0] = jnp.dot(_gelu_tanh(h).astype(BF16), w2_ref[...].astype(BF16),
                              preferred_element_type=F32).astype(o_ref.dtype)


def _compress(proj3, pk, pv, w1k, w1v, w2k, w2v):
    b = proj3.shape[0]
    half = CMP_STRIDE * HEAD_DIM
    xspec = lambda off: pl.BlockSpec((1, SEQ, HEAD_DIM), lambda i, g: (i, 0, off // HEAD_DIM + g))
    pspec = pl.BlockSpec((2, half), lambda i, g: (0, 0))
    w1spec = pl.BlockSpec((2, half, CMP_HIDDEN), lambda i, g: (0, 0, 0))
    w2spec = pl.BlockSpec((CMP_HIDDEN, HEAD_DIM), lambda i, g: (0, 0))
    ospec = pl.BlockSpec((1, 1, N_CHUNK, HEAD_DIM), lambda i, g: (i, g, 0, 0))
    oshape = jax.ShapeDtypeStruct((b, N_KV_GROUPS, N_CHUNK, HEAD_DIM), BF16)
    return pl.pallas_call(
        _compress_kernel,
        grid=(b, N_KV_GROUPS),
        in_specs=[xspec(OFF_KC), xspec(OFF_VC), pspec, pspec, w1spec, w1spec, w2spec, w2spec],
        out_specs=[ospec, ospec],
        out_shape=[oshape, oshape],
        scratch_shapes=[pltpu.VMEM((SEQ // 2, HEAD_DIM), jnp.uint32), pltpu.VMEM((N_CHUNK, half), F32)],
        compiler_params=_params("parallel", "parallel"),
        name="compress_kv",
    )(proj3, proj3, pk, pv, w1k, w1v, w2k, w2v)


def _attn_kernel(q_ref, ks_ref, vs_ref, kw_ref, vw_ref, kc_ref, vc_ref, gl_ref, ovl_ref, blk_ref, o_ref,
                 s_ref, mx_ref, acc_ref, sw_ref, ow_ref, *, tq):
    tk = tq
    grp = pl.program_id(1)
    qi = pl.program_id(2)
    q0 = qi * tq
    nh = HEADS_PER_GROUP
    rows = nh * tq
    nt = (((1,), (1,)), ((), ()))

    q = q_ref[0]
    qs = jnp.concatenate([q[:, r * HEAD_DIM:(r + 1) * HEAD_DIM] for r in range(nh)], axis=0)

    n_win = WINDOW // tk + 1
    wk0 = pl.multiple_of(jnp.maximum(qi - (n_win - 1), 0) * tk, tk)
    sc = lax.dot_general(qs, kw_ref[0, pl.ds(wk0, n_win * tk), :], nt, preferred_element_type=F32)
    key = wk0 + lax.broadcasted_iota(jnp.int32, (tq, n_win * tk), 1)
    t_abs = q0 + lax.broadcasted_iota(jnp.int32, (tq, n_win * tk), 0)
    win_bias = jnp.where((key <= t_abs) & (key > t_abs - WINDOW), 0.0, NEG_INF)
    m_win = None
    for i in range(n_win):
        part = sc[:, i * tk:(i + 1) * tk].reshape(nh, tq, tk) + win_bias[None, :, i * tk:(i + 1) * tk]
        part = part.reshape(rows, tk)
        sw_ref[i] = part
        for c in range(tk // LANES):
            piece = part[:, c * LANES:(c + 1) * LANES]
            m_win = piece if m_win is None else jnp.maximum(m_win, piece)
    n_pad = jnp.maximum(WINDOW - 1 - (q0 + lax.broadcasted_iota(jnp.int32, (tq, LANES), 0)), 0).astype(F32)
    n_pad = jnp.concatenate([n_pad] * nh, axis=0)
    m_win = jnp.broadcast_to(jnp.max(m_win, axis=-1, keepdims=True), (rows, LANES))
    m_win = jnp.where(n_pad > 0.0, jnp.maximum(m_win, 0.0), m_win)
    m_wide = jnp.concatenate([m_win] * (tk // LANES), axis=1)
    pr = jnp.concatenate([jnp.exp2(sw_ref[i] - m_wide).astype(BF16) for i in range(n_win)], axis=1)
    v_ext = jnp.concatenate([vw_ref[0, pl.ds(wk0, n_win * tk), :], jnp.ones((n_win * tk, LANES), BF16)], axis=1)
    acc_w = jnp.dot(pr, v_ext, preferred_element_type=F32)
    pad_term = n_pad * jnp.exp2(jnp.where(n_pad > 0.0, -m_win, 0.0))
    ow_ref[...] = acc_w[:, :HEAD_DIM] * (1.0 / (acc_w[:, HEAD_DIM:] + pad_term))

    s = lax.dot_general(qs, kc_ref[0, 0], nt, preferred_element_type=F32).reshape(nh, tq, N_CHUNK)
    n_idx = lax.broadcasted_iota(jnp.int32, (tq, N_CHUNK), 1)
    t_idx = q0 + lax.broadcasted_iota(jnp.int32, (tq, N_CHUNK), 0)
    vis = ((n_idx * CMP_STRIDE + (CMP_BLOCK - 1) <= t_idx) & (n_idx < N_CMP))[None]
    sm = jnp.where(vis, s, NEG_INF)
    mx = jnp.maximum(jnp.max(sm, axis=-1, keepdims=True), 0.5 * NEG_INF)
    e = jnp.exp2(sm - mx)
    p = e * (1.0 / jnp.maximum(jnp.sum(e, axis=-1, keepdims=True), 1e-30))
    o_cmp = jnp.dot(p.reshape(rows, N_CHUNK).astype(BF16), vc_ref[0, 0], preferred_element_type=F32)

    p_sum = p[0] + p[1] + p[2] + p[3]
    p_hi = p_sum.astype(BF16)
    rem = p_sum - p_hi.astype(F32)
    p_mid = rem.astype(BF16)
    p_lo = (rem - p_mid.astype(F32)).astype(BF16)
    ovl = ovl_ref[...]
    imp = (lax.dot_general(ovl, p_hi, nt, preferred_element_type=F32)
           + lax.dot_general(ovl, p_mid, nt, preferred_element_type=F32)
           + lax.dot_general(ovl, p_lo, nt, preferred_element_type=F32))

    j_idx = lax.broadcasted_iota(jnp.int32, (N_SLC, tq), 0)
    cur = jnp.right_shift(q0 + lax.broadcasted_iota(jnp.int32, (N_SLC, tq), 1), int(np.log2(SLC_BLOCK)))
    valid = j_idx <= cur
    forced = (j_idx == 0) | (j_idx == cur) | (j_idx == cur - 1)
    score = jnp.where(forced & valid, FORCED_SCORE, jnp.where(valid, imp, MASKED_SCORE))
    rank = jnp.zeros((N_SLC, tq), jnp.int32)
    for k in range(N_SLC):
        sk = score[k:k + 1, :]
        beats = (sk > score) | ((sk == score) & (j_idx > k))
        rank = rank + beats.astype(jnp.int32)
    unsel_t = jnp.where(rank < N_SELECT, 0.0, 1.0)
    unsel_t = jnp.concatenate([unsel_t, jnp.zeros((LANES - N_SLC, tq), F32)], axis=0)
    unsel = jnp.transpose(unsel_t).astype(BF16)

    q_slc = jnp.concatenate([qs, jnp.concatenate([unsel] * nh, axis=0)], axis=1)

    local_q = lax.broadcasted_iota(jnp.int32, (tq, tk), 0)
    local_k = lax.broadcasted_iota(jnp.int32, (tq, tk), 1)
    causal_bias = jnp.where(local_k <= local_q, 0.0, NEG_INF)

    def pass1(qa, k_ref, with_blocks, kt0, slot0, biases, init):
        n = len(biases)
        k0 = pl.multiple_of(kt0 * tk, tk)
        k = k_ref[0, pl.ds(k0, n * tk), :]
        if with_blocks:
            k = jnp.concatenate([k, blk_ref[pl.ds(k0, n * tk), :]], axis=1)
        sc = lax.dot_general(qa, k, nt, preferred_element_type=F32)
        m = None
        for i, bias in enumerate(biases):
            part = sc[:, i * tk:(i + 1) * tk]
            if bias is not None:
                part = (part.reshape(nh, tq, tk) + bias[None]).reshape(rows, tk)
            s_ref[slot0 + i] = part
            for c in range(tk // LANES):
                piece = part[:, c * LANES:(c + 1) * LANES]
                m = piece if m is None else jnp.maximum(m, piece)
        mx_ref[...] = m if init else jnp.maximum(mx_ref[...], m)

    def pass2(v_ref, kt0, slot0, n, init):
        k0 = pl.multiple_of(kt0 * tk, tk)
        v_ext = jnp.concatenate([v_ref[0, pl.ds(k0, n * tk), :], jnp.ones((n * tk, LANES), BF16)], axis=1)
        m_rep = mx_ref[...]
        m_wide = jnp.concatenate([m_rep] * (tk // LANES), axis=1)
        pr = jnp.concatenate([jnp.exp2(s_ref[slot0 + i] - m_wide).astype(BF16) for i in range(n)], axis=1)
        d = jnp.dot(pr, v_ext, preferred_element_type=F32)
        if init:
            acc_ref[...] = d
        else:
            acc_ref[...] += d

    def spread_row_max():
        mx_ref[...] = jnp.broadcast_to(jnp.max(mx_ref[...], axis=-1, keepdims=True), (rows, LANES))

    pairs = jnp.right_shift(qi, 1)
    odd = jnp.bitwise_and(qi, 1) == 1

    pass1(q_slc, ks_ref, True, qi, qi, [causal_bias], True)

    def slc_pass1(i, carry):
        pass1(q_slc, ks_ref, True, 2 * i, 2 * i, [None, None], False)
        return carry

    lax.fori_loop(0, pairs, slc_pass1, 0)

    @pl.when(odd)
    def _():
        pass1(q_slc, ks_ref, True, qi - 1, qi - 1, [None], False)

    spread_row_max()
    pass2(vs_ref, qi, qi, 1, True)

    def slc_pass2(i, carry):
        pass2(vs_ref, 2 * i, 2 * i, 2, False)
        return carry

    lax.fori_loop(0, pairs, slc_pass2, 0)

    @pl.when(odd)
    def _():
        pass2(vs_ref, qi - 1, qi - 1, 1, False)

    acc = acc_ref[...]
    o_slc = acc[:, :HEAD_DIM] * (1.0 / acc[:, HEAD_DIM:])

    o_win = ow_ref[...]

    gates = jax.nn.sigmoid(gl_ref[0])
    gates = pltpu.roll(gates, (grp * (LANES - nh * N_NSA_BRANCHES)) % LANES, axis=1)
    for r in range(nh):
        sl = slice(r * tq, (r + 1) * tq)
        c = r * N_NSA_BRANCHES
        o_r = (gates[:, c:c + 1] * o_cmp[sl] + gates[:, c + 1:c + 2] * o_slc[sl] + gates[:, c + 2:c + 3] * o_win[sl])
        o_ref[0, :, r * HEAD_DIM:(r + 1) * HEAD_DIM] = o_r.astype(o_ref.dtype)


def _attention(proj3, gate_logits3, k_cmp, v_cmp, overlap_t, block_cols, *, tq=ATTN_TILE):
    b = proj3.shape[0]
    gw = HEADS_PER_GROUP * HEAD_DIM
    kv = lambda off: pl.BlockSpec((1, SEQ, HEAD_DIM), lambda i, g, t: (i, 0, off // HEAD_DIM + g))
    cmp_spec = pl.BlockSpec((1, 1, N_CHUNK, HEAD_DIM), lambda i, g, t: (i, g, 0, 0))
    rows = HEADS_PER_GROUP * tq
    return pl.pallas_call(
        functools.partial(_attn_kernel, tq=tq),
        grid=(b, N_KV_GROUPS, SEQ // tq),
        in_specs=[
            pl.BlockSpec((1, tq, gw), lambda i, g, t: (i, t, OFF_Q // gw + g)),
            kv(OFF_KS), kv(OFF_VS), kv(OFF_KW), kv(OFF_VW),
            cmp_spec, cmp_spec,
            pl.BlockSpec((1, tq, LANES), lambda i, g, t: (i, t, 0)),
            pl.BlockSpec((N_SLC, N_CHUNK), lambda i, g, t: (0, 0)),
            pl.BlockSpec((SEQ, LANES), lambda i, g, t: (0, 0)),
        ],
        out_specs=pl.BlockSpec((1, tq, gw), lambda i, g, t: (i, t, g)),
        out_shape=jax.ShapeDtypeStruct((b, SEQ, D_Q), BF16),
        scratch_shapes=[
            pltpu.VMEM((SEQ // tq, rows, tq), F32),
            pltpu.VMEM((rows, LANES), F32),
            pltpu.VMEM((rows, 2 * HEAD_DIM), F32),
            pltpu.VMEM((WINDOW // tq + 1, rows, tq), F32),
            pltpu.VMEM((rows, HEAD_DIM), F32),
        ],
        compiler_params=_params("parallel", "parallel", "arbitrary"),
        name="nsa_attention",
    )(proj3, proj3, proj3, proj3, proj3, k_cmp, v_cmp, gate_logits3, overlap_t, block_cols)


def _conv_kernel(a_ref, b_ref, ah_ref, bh_ref, w_ref, bias_ref, lg_ref, lb_ref, o_ref, u_ref, y_ref, *, ts):
    nc = D_CONV // LANES
    first = pl.program_id(1) == 0
    u_main = a_ref[0].astype(F32) * jax.nn.sigmoid(b_ref[0].astype(F32))
    u_halo = ah_ref[0].astype(F32) * jax.nn.sigmoid(bh_ref[0].astype(F32))
    u_halo = jnp.where(first, 0.0, u_halo)
    for c in range(nc):
        u_ref[c, 0:HALO, :] = u_halo[:, c * LANES:(c + 1) * LANES]
        u_ref[c, HALO:HALO + ts, :] = u_main[:, c * LANES:(c + 1) * LANES]

    rc = 64
    base = HALO - (CONV_WIDTH - 1)

    def chunk_body(c, carry):
        for r0 in range(0, ts, rc):
            acc = jnp.zeros((rc, LANES), F32)
            for j in range(CONV_WIDTH):
                acc = acc + u_ref[c, pl.ds(base + r0 + j, rc), :] * w_ref[c, j:j + 1, :]
            y_ref[c, r0:r0 + rc, :] = acc
        return carry

    lax.fori_loop(0, nc, chunk_body, 0)

    y = jnp.concatenate([y_ref[c] for c in range(nc)], axis=1) + bias_ref[...]
    mu = jnp.mean(y, axis=-1, keepdims=True)
    d = y - mu
    var = jnp.mean(d * d, axis=-1, keepdims=True)
    z = d * lax.rsqrt(var + LN_EPS) * lg_ref[...] + lb_ref[...]
    o_ref[0] = (z * jax.nn.sigmoid(z)).astype(o_ref.dtype)


def _conformer_conv(proj3, w_chunks, b_dw, ln_g, ln_b, *, ts=256):
    b = proj3.shape[0]
    nc = D_CONV // LANES
    per = ts // HALO
    main = lambda off: pl.BlockSpec((1, ts, D_CONV), lambda i, t: (i, t, off // D_CONV))
    halo = lambda off: pl.BlockSpec((1, HALO, D_CONV), lambda i, t: (i, jnp.maximum(t * per - 1, 0), off // D_CONV))
    vec = pl.BlockSpec((1, D_CONV), lambda i, t: (0, 0))
    return pl.pallas_call(
        functools.partial(_conv_kernel, ts=ts),
        grid=(b, SEQ // ts),
        in_specs=[main(OFF_GLU_A), main(OFF_GLU_B), halo(OFF_GLU_A), halo(OFF_GLU_B),
                  pl.BlockSpec((nc, HALO, LANES), lambda i, t: (0, 0, 0)), vec, vec, vec],
        out_specs=pl.BlockSpec((1, ts, D_CONV), lambda i, t: (i, t, 0)),
        out_shape=jax.ShapeDtypeStruct((b, SEQ, D_CONV), BF16),
        scratch_shapes=[pltpu.VMEM((nc, HALO + ts, LANES), F32), pltpu.VMEM((nc, ts, LANES), F32)],
        compiler_params=_params("parallel", "arbitrary"),
        name="conformer_conv",
    )(proj3, proj3, proj3, proj3, w_chunks, b_dw.reshape(1, D_CONV), ln_g.reshape(1, D_CONV), ln_b.reshape(1, D_CONV))


def _overlap_t():
    cmp_start = np.arange(N_CHUNK) * CMP_STRIDE
    slc_start = np.arange(N_SLC) * SLC_BLOCK
    ov = ((cmp_start[None, :] < slc_start[:, None] + SLC_BLOCK) & (cmp_start[None, :] + CMP_BLOCK > slc_start[:, None])
          & (np.arange(N_CHUNK)[None, :] < N_CMP))
    return jnp.asarray(ov.astype(np.float32), dtype=BF16)


def _block_cols():
    own = (np.arange(SEQ)[:, None] // SLC_BLOCK) == np.arange(LANES)[None, :]
    return jnp.asarray(np.where(own, -MASK_BIG, 0.0).astype(np.float32), dtype=BF16)


def _layer(x, norm_mix_pre, w_in, pos_cmp_k, w_cmp_k1, w_cmp_k2, pos_cmp_v, w_cmp_v1, w_cmp_v2,
           w_dw, b_dw, ln_conv_g, ln_conv_b, w_conv_out, w_attn_out, w_out, norm_mix_post,
           norm_mlp_pre, w_up, w_down, norm_mlp_post):
    b, s, d = x.shape
    m = b * s
    x2 = x.reshape(m, d)
    half = CMP_STRIDE * HEAD_DIM

    w_in_t = jnp.swapaxes(w_in, 0, 1)
    col_scale = jnp.ones((D_MAIN,), F32).at[OFF_Q:OFF_Q + D_Q].set(HEAD_DIM ** -0.5 * LOG2_E)

    u, gate_logits = _rmsnorm_and_branch_gates(x2, norm_mix_pre, w_in_t, D_MAIN)
    proj = _matmul_nt(u, w_in_t, 0, D_MAIN, BF16, name="in_proj", col_scale=col_scale)
    proj3 = proj.reshape(b, s, D_MAIN)

    w_chunks = jnp.pad(w_dw.reshape(CONV_WIDTH, D_CONV), ((0, HALO - CONV_WIDTH), (0, 0)))
    w_chunks = w_chunks.reshape(HALO, D_CONV // LANES, LANES).transpose(1, 0, 2)
    gates_ab = _matmul_nt(u, w_in_t, D_MAIN + N_GATE, 2 * D_MODEL, BF16, name="in_proj_merge_gates")
    conv = _conformer_conv(proj3, w_chunks, b_dw, ln_conv_g, ln_conv_b)

    k_cmp, v_cmp = _compress(
        proj3, pos_cmp_k.reshape(2, half), pos_cmp_v.reshape(2, half),
        w_cmp_k1.reshape(2, half, CMP_HIDDEN), w_cmp_v1.reshape(2, half, CMP_HIDDEN), w_cmp_k2, w_cmp_v2)

    attn = _attention(proj3, gate_logits.reshape(b, s, LANES), k_cmp, v_cmp, _overlap_t(), _block_cols())

    merged = _merge(conv.reshape(m, D_CONV), attn.reshape(m, D_Q), w_conv_out, w_attn_out, gates_ab)
    z = _matmul(merged, w_out, BF16, name="out_proj")
    x1, h = _post_mix(x2, z, norm_mix_post, norm_mlp_pre)

    hidden = _matmul(h, w_up, BF16, name="mlp_up", relu2=True)
    y = _matmul_ktiled(hidden, w_down, BF16, bm=2048, bn=1024, bk=1024, name="mlp_down")
    return _post_mlp(x1, y, norm_mlp_post).reshape(b, s, d)


def kernel(x, norm_mix_pre, w_in, pos_cmp_k, w_cmp_k1, w_cmp_k2, pos_cmp_v, w_cmp_v1, w_cmp_v2, w_dw, b_dw,
           ln_conv_g, ln_conv_b, w_conv_out, w_attn_out, w_out, norm_mix_post, norm_mlp_pre, w_up, w_down,
           norm_mlp_post):
    for l in range(norm_mix_pre.shape[0]):
        x = _layer(x, norm_mix_pre[l], w_in[l], pos_cmp_k[l], w_cmp_k1[l], w_cmp_k2[l], pos_cmp_v[l], w_cmp_v1[l],
                   w_cmp_v2[l], w_dw[l], b_dw[l], ln_conv_g[l], ln_conv_b[l], w_conv_out[l], w_attn_out[l], w_out[l],
                   norm_mix_post[l], norm_mlp_pre[l], w_up[l], w_down[l], norm_mlp_post[l])
    return x
```

```python
import functools

import numpy as np
import jax
import jax.numpy as jnp
from jax import lax
from jax.experimental import pallas as pl
from jax.experimental.pallas import tpu as pltpu

D_MODEL = 4096
SEQ = 2048
D_CONV = D_MODEL // 2
CONV_WIDTH = 31
HEAD_DIM = 128
N_HEADS = 16
N_KV_GROUPS = 4
HEADS_PER_GROUP = N_HEADS // N_KV_GROUPS
CMP_BLOCK = 32
CMP_STRIDE = 16
CMP_HIDDEN = 2 * HEAD_DIM
SLC_BLOCK = 64
N_SELECT = 16
WINDOW = 512
N_NSA_BRANCHES = 3
D_Q = N_HEADS * HEAD_DIM
D_KV = N_KV_GROUPS * HEAD_DIM
N_GATE = N_NSA_BRANCHES * N_HEADS
D_MAIN = 2 * D_CONV + D_Q + 6 * D_KV
N_CHUNK = SEQ // CMP_STRIDE
N_CMP = N_CHUNK - CMP_BLOCK // CMP_STRIDE + 1
N_SLC = SEQ // SLC_BLOCK

NORM_EPS = 1e-6
LN_EPS = 1e-5
FORCED_SCORE = 1e4
MASKED_SCORE = -1e4
NEG_INF = -1e30
MASK_BIG = 2.0 ** 100
LOG2_E = float(np.log2(np.e))

LANES = 128
SUBLANES = 8
HALO = 32
ATTN_TILE = 256
MM_BM = 2048
MM_BN = 512
VMEM_LIMIT = 56 * 1024 * 1024

OFF_GLU_A = 0
OFF_GLU_B = D_CONV
OFF_Q = 2 * D_CONV
OFF_KC = OFF_Q + D_Q
OFF_VC = OFF_KC + D_KV
OFF_KS = OFF_VC + D_KV
OFF_VS = OFF_KS + D_KV
OFF_KW = OFF_VS + D_KV
OFF_VW = OFF_KW + D_KV

F32 = jnp.float32
BF16 = jnp.bfloat16


def _params(*sem):
    return pltpu.CompilerParams(dimension_semantics=sem, vmem_limit_bytes=VMEM_LIMIT)


def _rmsnorm_gates_kernel(x_ref, g_ref, wt_ref, u_ref, o_ref):
    x = x_ref[...]
    y = x * lax.rsqrt(jnp.mean(x * x, axis=-1, keepdims=True) + NORM_EPS)
    u = (y * g_ref[...]).astype(u_ref.dtype)
    u_ref[...] = u
    o_ref[...] = lax.dot_general(u, wt_ref[...].astype(BF16), (((1,), (1,)), ((), ())), preferred_element_type=F32)


def _rmsnorm_and_branch_gates(x, g, wt, row0, bm=512):
    m, d = x.shape
    return pl.pallas_call(
        _rmsnorm_gates_kernel,
        grid=(m // bm,),
        in_specs=[pl.BlockSpec((bm, d), lambda i: (i, 0)), pl.BlockSpec((1, d), lambda i: (0, 0)),
                  pl.BlockSpec((pl.Element(LANES), pl.Element(d)), lambda i: (row0, 0))],
        out_specs=[pl.BlockSpec((bm, d), lambda i: (i, 0)), pl.BlockSpec((bm, LANES), lambda i: (i, 0))],
        out_shape=[jax.ShapeDtypeStruct((m, d), BF16), jax.ShapeDtypeStruct((m, LANES), F32)],
        compiler_params=_params("parallel"),
        name="rmsnorm_in_branch_gates",
    )(x, g.reshape(1, d), wt)


def _post_mix_kernel(x_ref, z_ref, g1_ref, g2_ref, x1_ref, h_ref):
    z = z_ref[...].astype(F32)
    zn = z * lax.rsqrt(jnp.mean(z * z, axis=-1, keepdims=True) + NORM_EPS)
    x1 = x_ref[...] + zn * g1_ref[...]
    x1_ref[...] = x1
    hn = x1 * lax.rsqrt(jnp.mean(x1 * x1, axis=-1, keepdims=True) + NORM_EPS)
    h_ref[...] = (hn * g2_ref[...]).astype(h_ref.dtype)


def _post_mix(x, z, g1, g2, bm=256):
    m, d = x.shape
    row = pl.BlockSpec((bm, d), lambda i: (i, 0))
    vec = pl.BlockSpec((1, d), lambda i: (0, 0))
    return pl.pallas_call(
        _post_mix_kernel,
        grid=(m // bm,),
        in_specs=[row, row, vec, vec],
        out_specs=[row, row],
        out_shape=[jax.ShapeDtypeStruct((m, d), F32), jax.ShapeDtypeStruct((m, d), BF16)],
        compiler_params=_params("parallel"),
        name="post_mix_norm",
    )(x, z, g1.reshape(1, d), g2.reshape(1, d))


def _post_mlp_kernel(x_ref, y_ref, g_ref, o_ref):
    y = y_ref[...].astype(F32)
    yn = y * lax.rsqrt(jnp.mean(y * y, axis=-1, keepdims=True) + NORM_EPS)
    o_ref[...] = x_ref[...] + yn * g_ref[...]


def _post_mlp(x, y, g, bm=512):
    m, d = x.shape
    row = pl.BlockSpec((bm, d), lambda i: (i, 0))
    return pl.pallas_call(
        _post_mlp_kernel,
        grid=(m // bm,),
        in_specs=[row, row, pl.BlockSpec((1, d), lambda i: (0, 0))],
        out_specs=row,
        out_shape=jax.ShapeDtypeStruct((m, d), F32),
        compiler_params=_params("parallel"),
        name="post_mlp_norm",
    )(x, y, g.reshape(1, d))


_NT_DIMS = (((1,), (1,)), ((), ()))


def _resident_rows_spec(bm, k):
    return pl.BlockSpec((bm, k), lambda i, j: (i, 0), pipeline_mode=pl.Buffered(1))


def _mm_nt_scale_kernel(a_ref, wt_ref, s_ref, o_ref):
    acc = lax.dot_general(a_ref[...], wt_ref[...].astype(BF16), _NT_DIMS, preferred_element_type=F32)
    o_ref[...] = (acc * s_ref[...]).astype(o_ref.dtype)


def _mm_nt_kernel(a_ref, wt_ref, o_ref):
    acc = lax.dot_general(a_ref[...], wt_ref[...].astype(BF16), _NT_DIMS, preferred_element_type=F32)
    o_ref[...] = acc.astype(o_ref.dtype)


def _matmul_nt(a, wt, row0, n, out_dtype, *, name, bm=MM_BM, bn=MM_BN, col_scale=None):
    m, k = a.shape
    in_specs = [_resident_rows_spec(bm, k),
                pl.BlockSpec((pl.Element(bn), pl.Element(k)),
                             lambda i, j: ((row0 // SUBLANES + j * (bn // SUBLANES)) * SUBLANES, 0))]
    args = [a, wt]
    body = _mm_nt_kernel
    if col_scale is not None:
        body = _mm_nt_scale_kernel
        in_specs.append(pl.BlockSpec((1, bn), lambda i, j: (0, j)))
        args.append(col_scale.reshape(1, n))
    return pl.pallas_call(
        body,
        grid=(m // bm, n // bn),
        in_specs=in_specs,
        out_specs=pl.BlockSpec((bm, bn), lambda i, j: (i, j)),
        out_shape=jax.ShapeDtypeStruct((m, n), out_dtype),
        compiler_params=_params("parallel", "arbitrary"),
        name=name,
    )(*args)


def _mm_plain_kernel(a_ref, w_ref, o_ref):
    o_ref[...] = jnp.dot(a_ref[...], w_ref[...].astype(BF16), preferred_element_type=F32).astype(o_ref.dtype)


def _mm_relu2_kernel(a_ref, w_ref, o_ref):
    acc = jnp.dot(a_ref[...], w_ref[...].astype(BF16), preferred_element_type=F32)
    r = jnp.maximum(acc, 0.0)
    o_ref[...] = (r * r).astype(o_ref.dtype)


def _matmul(a, w, out_dtype, *, name, bm=MM_BM, bn=MM_BN, relu2=False):
    m, k = a.shape
    n = w.shape[1]
    return pl.pallas_call(
        _mm_relu2_kernel if relu2 else _mm_plain_kernel,
        grid=(m // bm, n // bn),
        in_specs=[_resident_rows_spec(bm, k), pl.BlockSpec((k, bn), lambda i, j: (0, j))],
        out_specs=pl.BlockSpec((bm, bn), lambda i, j: (i, j)),
        out_shape=jax.ShapeDtypeStruct((m, n), out_dtype),
        compiler_params=_params("parallel", "arbitrary"),
        name=name,
    )(a, w)


def _mm_kacc_kernel(a_ref, w_ref, o_ref, acc_ref):
    kk = pl.program_id(2)

    @pl.when(kk == 0)
    def _():
        acc_ref[...] = jnp.zeros_like(acc_ref)

    acc_ref[...] += jnp.dot(a_ref[...], w_ref[...].astype(BF16), preferred_element_type=F32)

    @pl.when(kk == pl.num_programs(2) - 1)
    def _():
        o_ref[...] = acc_ref[...].astype(o_ref.dtype)


def _matmul_ktiled(a, w, out_dtype, *, bk, name, bm=MM_BM, bn=MM_BN):
    m, k = a.shape
    n = w.shape[1]
    return pl.pallas_call(
        _mm_kacc_kernel,
        grid=(m // bm, n // bn, k // bk),
        in_specs=[pl.BlockSpec((bm, bk), lambda i, j, kk: (i, kk)), pl.BlockSpec((bk, bn), lambda i, j, kk: (kk, j))],
        out_specs=pl.BlockSpec((bm, bn), lambda i, j, kk: (i, j)),
        out_shape=jax.ShapeDtypeStruct((m, n), out_dtype),
        scratch_shapes=[pltpu.VMEM((bm, bn), F32)],
        compiler_params=_params("parallel", "parallel", "arbitrary"),
        name=name,
    )(a, w)


def _merge_kernel(c_ref, a_ref, wc_ref, wa_ref, ga_ref, gb_ref, o_ref):
    yc = jnp.dot(c_ref[...], wc_ref[...].astype(BF16), preferred_element_type=F32)
    ya = jnp.dot(a_ref[...], wa_ref[...].astype(BF16), preferred_element_type=F32)
    ga = jax.nn.sigmoid(ga_ref[...].astype(F32))
    gb = jax.nn.sigmoid(gb_ref[...].astype(F32))
    o_ref[...] = (ga * yc + gb * ya).astype(o_ref.dtype)


def _merge(conv_act, attn_act, wc, wa, gates_ab, *, bm=MM_BM // 2, bn=MM_BN):
    m, kc = conv_act.shape
    ka = attn_act.shape[1]
    n = wc.shape[1]
    jb = n // bn
    return pl.pallas_call(
        _merge_kernel,
        grid=(m // bm, n // bn),
        in_specs=[
            pl.BlockSpec((bm, kc), lambda i, j: (i, 0)),
            pl.BlockSpec((bm, ka), lambda i, j: (i, 0)),
            pl.BlockSpec((kc, bn), lambda i, j: (0, j)),
            pl.BlockSpec((ka, bn), lambda i, j: (0, j)),
            pl.BlockSpec((bm, bn), lambda i, j: (i, j)),
            pl.BlockSpec((bm, bn), lambda i, j: (i, jb + j)),
        ],
        out_specs=pl.BlockSpec((bm, bn), lambda i, j: (i, j)),
        out_shape=jax.ShapeDtypeStruct((m, n), BF16),
        compiler_params=_params("parallel", "arbitrary"),
        name="gated_merge",
    )(conv_act, attn_act, wc, wa, gates_ab, gates_ab)


def _gelu_tanh(x):
    return 0.5 * x * (1.0 + jnp.tanh(np.sqrt(2.0 / np.pi) * (x + 0.044715 * (x * x * x))))


def _compress_kernel(xk_ref, xv_ref, pk_ref, pv_ref, w1k_ref, w1v_ref, w2k_ref, w2v_ref, ok_ref, ov_ref,
                     pair_ref, flat_ref):
    for x_ref, p_ref, w1_ref, w2_ref, o_ref in ((xk_ref, pk_ref, w1k_ref, w2k_ref, ok_ref),
                                                (xv_ref, pv_ref, w1v_ref, w2v_ref, ov_ref)):
        pair_ref[...] = pltpu.bitcast(x_ref[0], jnp.uint32)
        for l in range(CMP_STRIDE):
            word = pair_ref[pl.ds(l // 2, N_CHUNK, stride=CMP_STRIDE // 2), :]
            bits = (word << 16) if l % 2 == 0 else (word & jnp.uint32(0xFFFF0000))
            flat_ref[:, l * HEAD_DIM:(l + 1) * HEAD_DIM] = pltpu.bitcast(bits, F32)
        x = flat_ref[...]
        first = jnp.dot((x + p_ref[0:1, :]).astype(BF16), w1_ref[0].astype(BF16), preferred_element_type=F32)
        second = jnp.dot((x + p_ref[1:2, :]).astype(BF16), w1_ref[1].astype(BF16), preferred_element_type=F32)
        h = first + pltpu.roll(second, N_CHUNK - 1, axis=0)
        o_ref[0, 0] = jnp.dot(_gelu_tanh(h).astype(BF16), w2_ref[...].astype(BF16),
                              preferred_element_type=F32).astype(o_ref.dtype)


def _compress(proj3, pk, pv, w1k, w1v, w2k, w2v):
    b = proj3.shape[0]
    half = CMP_STRIDE * HEAD_DIM
    xspec = lambda off: pl.BlockSpec((1, SEQ, HEAD_DIM), lambda i, g: (i, 0, off // HEAD_DIM + g))
    pspec = pl.BlockSpec((2, half), lambda i, g: (0, 0))
    w1spec = pl.BlockSpec((2, half, CMP_HIDDEN), lambda i, g: (0, 0, 0))
    w2spec = pl.BlockSpec((CMP_HIDDEN, HEAD_DIM), lambda i, g: (0, 0))
    ospec = pl.BlockSpec((1, 1, N_CHUNK, HEAD_DIM), lambda i, g: (i, g, 0, 0))
    oshape = jax.ShapeDtypeStruct((b, N_KV_GROUPS, N_CHUNK, HEAD_DIM), BF16)
    return pl.pallas_call(
        _compress_kernel,
        grid=(b, N_KV_GROUPS),
        in_specs=[xspec(OFF_KC), xspec(OFF_VC), pspec, pspec, w1spec, w1spec, w2spec, w2spec],
        out_specs=[ospec, ospec],
        out_shape=[oshape, oshape],
        scratch_shapes=[pltpu.VMEM((SEQ // 2, HEAD_DIM), jnp.uint32), pltpu.VMEM((N_CHUNK, half), F32)],
        compiler_params=_params("parallel", "parallel"),
        name="compress_kv",
    )(proj3, proj3, pk, pv, w1k, w1v, w2k, w2v)


def _attn_kernel(q_ref, ks_ref, vs_ref, kw_ref, vw_ref, kc_ref, vc_ref, gl_ref, ovl_ref, blk_ref, o_ref,
                 s_ref, mx_ref, acc_ref, sw_ref, ow_ref, *, tq):
    tk = tq
    grp = pl.program_id(1)
    qi = pl.program_id(2)
    q0 = qi * tq
    nh = HEADS_PER_GROUP
    rows = nh * tq
    nt = (((1,), (1,)), ((), ()))

    q = q_ref[0]
    qs = jnp.concatenate([q[:, r * HEAD_DIM:(r + 1) * HEAD_DIM] for r in range(nh)], axis=0)

    n_win = WINDOW // tk + 1
    wk0 = pl.multiple_of(jnp.maximum(qi - (n_win - 1), 0) * tk, tk)
    sc = lax.dot_general(qs, kw_ref[0, pl.ds(wk0, n_win * tk), :], nt, preferred_element_type=F32)
    key = wk0 + lax.broadcasted_iota(jnp.int32, (tq, n_win * tk), 1)
    t_abs = q0 + lax.broadcasted_iota(jnp.int32, (tq, n_win * tk), 0)
    win_bias = jnp.where((key <= t_abs) & (key > t_abs - WINDOW), 0.0, NEG_INF)
    m_win = None
    for i in range(n_win):
        part = sc[:, i * tk:(i + 1) * tk].reshape(nh, tq, tk) + win_bias[None, :, i * tk:(i + 1) * tk]
        part = part.reshape(rows, tk)
        sw_ref[i] = part
        for c in range(tk // LANES):
            piece = part[:, c * LANES:(c + 1) * LANES]
            m_win = piece if m_win is None else jnp.maximum(m_win, piece)
    n_pad = jnp.maximum(WINDOW - 1 - (q0 + lax.broadcasted_iota(jnp.int32, (tq, LANES), 0)), 0).astype(F32)
    n_pad = jnp.concatenate([n_pad] * nh, axis=0)
    m_win = jnp.broadcast_to(jnp.max(m_win, axis=-1, keepdims=True), (rows, LANES))
    m_win = jnp.where(n_pad > 0.0, jnp.maximum(m_win, 0.0), m_win)
    m_wide = jnp.concatenate([m_win] * (tk // LANES), axis=1)
    pr = jnp.concatenate([jnp.exp2(sw_ref[i] - m_wide).astype(BF16) for i in range(n_win)], axis=1)
    v_ext = jnp.concatenate([vw_ref[0, pl.ds(wk0, n_win * tk), :], jnp.ones((n_win * tk, LANES), BF16)], axis=1)
    acc_w = jnp.dot(pr, v_ext, preferred_element_type=F32)
    pad_term = n_pad * jnp.exp2(jnp.where(n_pad > 0.0, -m_win, 0.0))
    ow_ref[...] = acc_w[:, :HEAD_DIM] * (1.0 / (acc_w[:, HEAD_DIM:] + pad_term))

    s = lax.dot_general(qs, kc_ref[0, 0], nt, preferred_element_type=F32).reshape(nh, tq, N_CHUNK)
    n_idx = lax.broadcasted_iota(jnp.int32, (tq, N_CHUNK), 1)
    t_idx = q0 + lax.broadcasted_iota(jnp.int32, (tq, N_CHUNK), 0)
    vis = ((n_idx * CMP_STRIDE + (CMP_BLOCK - 1) <= t_idx) & (n_idx < N_CMP))[None]
    sm = jnp.where(vis, s, NEG_INF)
    mx = jnp.maximum(jnp.max(sm, axis=-1, keepdims=True), 0.5 * NEG_INF)
    e = jnp.exp2(sm - mx)
    p = e * (1.0 / jnp.maximum(jnp.sum(e, axis=-1, keepdims=True), 1e-30))
    o_cmp = jnp.dot(p.reshape(rows, N_CHUNK).astype(BF16), vc_ref[0, 0], preferred_element_type=F32)

    p_sum = p[0] + p[1] + p[2] + p[3]
    p_hi = p_sum.astype(BF16)
    rem = p_sum - p_hi.astype(F32)
    p_mid = rem.astype(BF16)
    p_lo = (rem - p_mid.astype(F32)).astype(BF16)
    ovl = ovl_ref[...]
    imp = (lax.dot_general(ovl, p_hi, nt, preferred_element_type=F32)
           + lax.dot_general(ovl, p_mid, nt, preferred_element_type=F32)
           + lax.dot_general(ovl, p_lo, nt, preferred_element_type=F32))

    j_idx = lax.broadcasted_iota(jnp.int32, (N_SLC, tq), 0)
    cur = jnp.right_shift(q0 + lax.broadcasted_iota(jnp.int32, (N_SLC, tq), 1), int(np.log2(SLC_BLOCK)))
    valid = j_idx <= cur
    forced = (j_idx == 0) | (j_idx == cur) | (j_idx == cur - 1)
    score = jnp.where(forced & valid, FORCED_SCORE, jnp.where(valid, imp, MASKED_SCORE))
    rank = jnp.zeros((N_SLC, tq), jnp.int32)
    for k in range(N_SLC):
        sk = score[k:k + 1, :]
        beats = (sk > score) | ((sk == score) & (j_idx > k))
        rank = rank + beats.astype(jnp.int32)
    unsel_t = jnp.where(rank < N_SELECT, 0.0, 1.0)
    unsel_t = jnp.concatenate([unsel_t, jnp.zeros((LANES - N_SLC, tq), F32)], axis=0)
    unsel = jnp.transpose(unsel_t).astype(BF16)

    q_slc = jnp.concatenate([qs, jnp.concatenate([unsel] * nh, axis=0)], axis=1)

    local_q = lax.broadcasted_iota(jnp.int32, (tq, tk), 0)
    local_k = lax.broadcasted_iota(jnp.int32, (tq, tk), 1)
    causal_bias = jnp.where(local_k <= local_q, 0.0, NEG_INF)

    def slc_pass1(n):
        k = jnp.concatenate([ks_ref[0, 0:n * tk, :], blk_ref[0:n * tk, :]], axis=1)
        sc = lax.dot_general(q_slc, k, nt, preferred_element_type=F32)
        m = None
        for i in range(n):
            part = sc[:, i * tk:(i + 1) * tk]
            if i == n - 1:
                part = (part.reshape(nh, tq, tk) + causal_bias[None]).reshape(rows, tk)
            s_ref[i] = part
            for c in range(tk // LANES):
                piece = part[:, c * LANES:(c + 1) * LANES]
                m = piece if m is None else jnp.maximum(m, piece)
        mx_ref[...] = m

    def slc_pass2(n):
        v_ext = jnp.concatenate([vs_ref[0, 0:n * tk, :], jnp.ones((n * tk, LANES), BF16)], axis=1)
        m_wide = jnp.concatenate([mx_ref[...]] * (tk // LANES), axis=1)
        pr = jnp.concatenate([jnp.exp2(s_ref[i] - m_wide).astype(BF16) for i in range(n)], axis=1)
        acc_ref[...] = jnp.dot(pr, v_ext, preferred_element_type=F32)

    n_q_tiles = SEQ // tq
    for n_tiles in range(1, n_q_tiles + 1):
        pl.when(qi == n_tiles - 1)(functools.partial(slc_pass1, n_tiles))

    mx_ref[...] = jnp.broadcast_to(jnp.max(mx_ref[...], axis=-1, keepdims=True), (rows, LANES))
    for n_tiles in range(1, n_q_tiles + 1):
        pl.when(qi == n_tiles - 1)(functools.partial(slc_pass2, n_tiles))

    acc = acc_ref[...]
    o_slc = acc[:, :HEAD_DIM] * (1.0 / acc[:, HEAD_DIM:])

    o_win = ow_ref[...]

    gates = jax.nn.sigmoid(gl_ref[0])
    gates = pltpu.roll(gates, (grp * (LANES - nh * N_NSA_BRANCHES)) % LANES, axis=1)
    for r in range(nh):
        sl = slice(r * tq, (r + 1) * tq)
        c = r * N_NSA_BRANCHES
        o_r = (gates[:, c:c + 1] * o_cmp[sl] + gates[:, c + 1:c + 2] * o_slc[sl] + gates[:, c + 2:c + 3] * o_win[sl])
        o_ref[0, :, r * HEAD_DIM:(r + 1) * HEAD_DIM] = o_r.astype(o_ref.dtype)


def _attention(proj3, gate_logits3, k_cmp, v_cmp, overlap_t, block_cols, *, tq=ATTN_TILE):
    b = proj3.shape[0]
    gw = HEADS_PER_GROUP * HEAD_DIM
    kv = lambda off: pl.BlockSpec((1, SEQ, HEAD_DIM), lambda i, g, t: (i, 0, off // HEAD_DIM + g))
    cmp_spec = pl.BlockSpec((1, 1, N_CHUNK, HEAD_DIM), lambda i, g, t: (i, g, 0, 0))
    rows = HEADS_PER_GROUP * tq
    return pl.pallas_call(
        functools.partial(_attn_kernel, tq=tq),
        grid=(b, N_KV_GROUPS, SEQ // tq),
        in_specs=[
            pl.BlockSpec((1, tq, gw), lambda i, g, t: (i, t, OFF_Q // gw + g)),
            kv(OFF_KS), kv(OFF_VS), kv(OFF_KW), kv(OFF_VW),
            cmp_spec, cmp_spec,
            pl.BlockSpec((1, tq, LANES), lambda i, g, t: (i, t, 0)),
            pl.BlockSpec((N_SLC, N_CHUNK), lambda i, g, t: (0, 0)),
            pl.BlockSpec((SEQ, LANES), lambda i, g, t: (0, 0)),
        ],
        out_specs=pl.BlockSpec((1, tq, gw), lambda i, g, t: (i, t, g)),
        out_shape=jax.ShapeDtypeStruct((b, SEQ, D_Q), BF16),
        scratch_shapes=[
            pltpu.VMEM((SEQ // tq, rows, tq), F32),
            pltpu.VMEM((rows, LANES), F32),
            pltpu.VMEM((rows, 2 * HEAD_DIM), F32),
            pltpu.VMEM((WINDOW // tq + 1, rows, tq), F32),
            pltpu.VMEM((rows, HEAD_DIM), F32),
        ],
        compiler_params=_params("parallel", "parallel", "arbitrary"),
        name="nsa_attention",
    )(proj3, proj3, proj3, proj3, proj3, k_cmp, v_cmp, gate_logits3, overlap_t, block_cols)


def _conv_kernel(a_ref, b_ref, ah_ref, bh_ref, w_ref, bias_ref, lg_ref, lb_ref, o_ref, u_ref, y_ref, *, ts):
    nc = D_CONV // LANES
    first = pl.program_id(1) == 0
    u_main = a_ref[0].astype(F32) * jax.nn.sigmoid(b_ref[0].astype(F32))
    u_halo = ah_ref[0].astype(F32) * jax.nn.sigmoid(bh_ref[0].astype(F32))
    u_halo = jnp.where(first, 0.0, u_halo)
    for c in range(nc):
        u_ref[c, 0:HALO, :] = u_halo[:, c * LANES:(c + 1) * LANES]
        u_ref[c, HALO:HALO + ts, :] = u_main[:, c * LANES:(c + 1) * LANES]

    rc = 64
    base = HALO - (CONV_WIDTH - 1)

    def chunk_body(c, carry):
        for r0 in range(0, ts, rc):
            acc = jnp.zeros((rc, LANES), F32)
            for j in range(CONV_WIDTH):
                acc = acc + u_ref[c, pl.ds(base + r0 + j, rc), :] * w_ref[c, j:j + 1, :]
            y_ref[c, r0:r0 + rc, :] = acc
        return carry

    lax.fori_loop(0, nc, chunk_body, 0)

    y = jnp.concatenate([y_ref[c] for c in range(nc)], axis=1) + bias_ref[...]
    mu = jnp.mean(y, axis=-1, keepdims=True)
    d = y - mu
    var = jnp.mean(d * d, axis=-1, keepdims=True)
    z = d * lax.rsqrt(var + LN_EPS) * lg_ref[...] + lb_ref[...]
    o_ref[0] = (z * jax.nn.sigmoid(z)).astype(o_ref.dtype)


def _conformer_conv(proj3, w_chunks, b_dw, ln_g, ln_b, *, ts=256):
    b = proj3.shape[0]
    nc = D_CONV // LANES
    per = ts // HALO
    main = lambda off: pl.BlockSpec((1, ts, D_CONV), lambda i, t: (i, t, off // D_CONV))
    halo = lambda off: pl.BlockSpec((1, HALO, D_CONV), lambda i, t: (i, jnp.maximum(t * per - 1, 0), off // D_CONV))
    vec = pl.BlockSpec((1, D_CONV), lambda i, t: (0, 0))
    return pl.pallas_call(
        functools.partial(_conv_kernel, ts=ts),
        grid=(b, SEQ // ts),
        in_specs=[main(OFF_GLU_A), main(OFF_GLU_B), halo(OFF_GLU_A), halo(OFF_GLU_B),
                  pl.BlockSpec((nc, HALO, LANES), lambda i, t: (0, 0, 0)), vec, vec, vec],
        out_specs=pl.BlockSpec((1, ts, D_CONV), lambda i, t: (i, t, 0)),
        out_shape=jax.ShapeDtypeStruct((b, SEQ, D_CONV), BF16),
        scratch_shapes=[pltpu.VMEM((nc, HALO + ts, LANES), F32), pltpu.VMEM((nc, ts, LANES), F32)],
        compiler_params=_params("parallel", "arbitrary"),
        name="conformer_conv",
    )(proj3, proj3, proj3, proj3, w_chunks, b_dw.reshape(1, D_CONV), ln_g.reshape(1, D_CONV), ln_b.reshape(1, D_CONV))


def _overlap_t():
    cmp_start = np.arange(N_CHUNK) * CMP_STRIDE
    slc_start = np.arange(N_SLC) * SLC_BLOCK
    ov = ((cmp_start[None, :] < slc_start[:, None] + SLC_BLOCK) & (cmp_start[None, :] + CMP_BLOCK > slc_start[:, None])
          & (np.arange(N_CHUNK)[None, :] < N_CMP))
    return jnp.asarray(ov.astype(np.float32), dtype=BF16)


def _block_cols():
    own = (np.arange(SEQ)[:, None] // SLC_BLOCK) == np.arange(LANES)[None, :]
    return jnp.asarray(np.where(own, -MASK_BIG, 0.0).astype(np.float32), dtype=BF16)


def _layer(x, norm_mix_pre, w_in, pos_cmp_k, w_cmp_k1, w_cmp_k2, pos_cmp_v, w_cmp_v1, w_cmp_v2,
           w_dw, b_dw, ln_conv_g, ln_conv_b, w_conv_out, w_attn_out, w_out, norm_mix_post,
           norm_mlp_pre, w_up, w_down, norm_mlp_post):
    b, s, d = x.shape
    m = b * s
    x2 = x.reshape(m, d)
    half = CMP_STRIDE * HEAD_DIM

    w_in_t = jnp.swapaxes(w_in, 0, 1)
    col_scale = jnp.ones((D_MAIN,), F32).at[OFF_Q:OFF_Q + D_Q].set(HEAD_DIM ** -0.5 * LOG2_E)

    u, gate_logits = _rmsnorm_and_branch_gates(x2, norm_mix_pre, w_in_t, D_MAIN)
    proj = _matmul_nt(u, w_in_t, 0, D_MAIN, BF16, name="in_proj", col_scale=col_scale)
    proj3 = proj.reshape(b, s, D_MAIN)

    w_chunks = jnp.pad(w_dw.reshape(CONV_WIDTH, D_CONV), ((0, HALO - CONV_WIDTH), (0, 0)))
    w_chunks = w_chunks.reshape(HALO, D_CONV // LANES, LANES).transpose(1, 0, 2)
    gates_ab = _matmul_nt(u, w_in_t, D_MAIN + N_GATE, 2 * D_MODEL, BF16, name="in_proj_merge_gates")
    conv = _conformer_conv(proj3, w_chunks, b_dw, ln_conv_g, ln_conv_b)

    k_cmp, v_cmp = _compress(
        proj3, pos_cmp_k.reshape(2, half), pos_cmp_v.reshape(2, half),
        w_cmp_k1.reshape(2, half, CMP_HIDDEN), w_cmp_v1.reshape(2, half, CMP_HIDDEN), w_cmp_k2, w_cmp_v2)

    attn = _attention(proj3, gate_logits.reshape(b, s, LANES), k_cmp, v_cmp, _overlap_t(), _block_cols())

    merged = _merge(conv.reshape(m, D_CONV), attn.reshape(m, D_Q), w_conv_out, w_attn_out, gates_ab)
    z = _matmul(merged, w_out, BF16, name="out_proj")
    x1, h = _post_mix(x2, z, norm_mix_post, norm_mlp_pre)

    hidden = _matmul(h, w_up, BF16, name="mlp_up", relu2=True)
    y = _matmul_ktiled(hidden, w_down, BF16, bm=2048, bn=1024, bk=1024, name="mlp_down")
    return _post_mlp(x1, y, norm_mlp_post).reshape(b, s, d)


def kernel(x, norm_mix_pre, w_in, pos_cmp_k, w_cmp_k1, w_cmp_k2, pos_cmp_v, w_cmp_v1, w_cmp_v2, w_dw, b_dw,
           ln_conv_g, ln_conv_b, w_conv_out, w_attn_out, w_out, norm_mix_post, norm_mlp_pre, w_up, w_down,
           norm_mlp_post):
    for l in range(norm_mix_pre.shape[0]):
        x = _layer(x, norm_mix_pre[l], w_in[l], pos_cmp_k[l], w_cmp_k1[l], w_cmp_k2[l], pos_cmp_v[l], w_cmp_v1[l],
                   w_cmp_v2[l], w_dw[l], b_dw[l], ln_conv_g[l], ln_conv_b[l], w_conv_out[l], w_attn_out[l], w_out[l],
                   norm_mix_post[l], norm_mlp_pre[l], w_up[l], w_down[l], norm_mlp_post[l])
    return x
```

```python
import functools

import numpy as np
import jax
import jax.numpy as jnp
from jax import lax
from jax.experimental import pallas as pl
from jax.experimental.pallas import tpu as pltpu

D_MODEL = 4096
SEQ = 2048
D_CONV = D_MODEL // 2
CONV_WIDTH = 31
HEAD_DIM = 128
N_HEADS = 16
N_KV_GROUPS = 4
HEADS_PER_GROUP = N_HEADS // N_KV_GROUPS
CMP_BLOCK = 32
CMP_STRIDE = 16
CMP_HIDDEN = 2 * HEAD_DIM
SLC_BLOCK = 64
N_SELECT = 16
WINDOW = 512
N_NSA_BRANCHES = 3
D_Q = N_HEADS * HEAD_DIM
D_KV = N_KV_GROUPS * HEAD_DIM
N_GATE = N_NSA_BRANCHES * N_HEADS
D_MAIN = 2 * D_CONV + D_Q + 6 * D_KV
N_CHUNK = SEQ // CMP_STRIDE
N_CMP = N_CHUNK - CMP_BLOCK // CMP_STRIDE + 1
N_SLC = SEQ // SLC_BLOCK

NORM_EPS = 1e-6
LN_EPS = 1e-5
FORCED_SCORE = 1e4
MASKED_SCORE = -1e4
NEG_INF = -1e30
MASK_BIG = 2.0 ** 100
LOG2_E = float(np.log2(np.e))

LANES = 128
SUBLANES = 8
HALO = 32
ATTN_TILE = 256
MM_BM = 2048
MM_BN = 512
VMEM_LIMIT = 56 * 1024 * 1024

OFF_GLU_A = 0
OFF_GLU_B = D_CONV
OFF_Q = 2 * D_CONV
OFF_KC = OFF_Q + D_Q
OFF_VC = OFF_KC + D_KV
OFF_KS = OFF_VC + D_KV
OFF_VS = OFF_KS + D_KV
OFF_KW = OFF_VS + D_KV
OFF_VW = OFF_KW + D_KV

F32 = jnp.float32
BF16 = jnp.bfloat16


def _params(*sem):
    return pltpu.CompilerParams(dimension_semantics=sem, vmem_limit_bytes=VMEM_LIMIT)


def _rmsnorm_gates_kernel(x_ref, g_ref, wt_ref, u_ref, o_ref):
    x = x_ref[...]
    y = x * lax.rsqrt(jnp.mean(x * x, axis=-1, keepdims=True) + NORM_EPS)
    u = (y * g_ref[...]).astype(u_ref.dtype)
    u_ref[...] = u
    o_ref[...] = lax.dot_general(u, wt_ref[...].astype(BF16), (((1,), (1,)), ((), ())), preferred_element_type=F32)


def _rmsnorm_and_branch_gates(x, g, wt, row0, bm=512):
    m, d = x.shape
    return pl.pallas_call(
        _rmsnorm_gates_kernel,
        grid=(m // bm,),
        in_specs=[pl.BlockSpec((bm, d), lambda i: (i, 0)), pl.BlockSpec((1, d), lambda i: (0, 0)),
                  pl.BlockSpec((pl.Element(LANES), pl.Element(d)), lambda i: (row0, 0))],
        out_specs=[pl.BlockSpec((bm, d), lambda i: (i, 0)), pl.BlockSpec((bm, LANES), lambda i: (i, 0))],
        out_shape=[jax.ShapeDtypeStruct((m, d), BF16), jax.ShapeDtypeStruct((m, LANES), F32)],
        compiler_params=_params("parallel"),
        name="rmsnorm_in_branch_gates",
    )(x, g.reshape(1, d), wt)


def _post_mix_kernel(x_ref, z_ref, g1_ref, g2_ref, x1_ref, h_ref):
    z = z_ref[...].astype(F32)
    zn = z * lax.rsqrt(jnp.mean(z * z, axis=-1, keepdims=True) + NORM_EPS)
    x1 = x_ref[...] + zn * g1_ref[...]
    x1_ref[...] = x1
    hn = x1 * lax.rsqrt(jnp.mean(x1 * x1, axis=-1, keepdims=True) + NORM_EPS)
    h_ref[...] = (hn * g2_ref[...]).astype(h_ref.dtype)


def _post_mix(x, z, g1, g2, bm=256):
    m, d = x.shape
    row = pl.BlockSpec((bm, d), lambda i: (i, 0))
    vec = pl.BlockSpec((1, d), lambda i: (0, 0))
    return pl.pallas_call(
        _post_mix_kernel,
        grid=(m // bm,),
        in_specs=[row, row, vec, vec],
        out_specs=[row, row],
        out_shape=[jax.ShapeDtypeStruct((m, d), F32), jax.ShapeDtypeStruct((m, d), BF16)],
        compiler_params=_params("parallel"),
        name="post_mix_norm",
    )(x, z, g1.reshape(1, d), g2.reshape(1, d))


def _post_mlp_kernel(x_ref, y_ref, g_ref, o_ref):
    y = y_ref[...].astype(F32)
    yn = y * lax.rsqrt(jnp.mean(y * y, axis=-1, keepdims=True) + NORM_EPS)
    o_ref[...] = x_ref[...] + yn * g_ref[...]


def _post_mlp(x, y, g, bm=512):
    m, d = x.shape
    row = pl.BlockSpec((bm, d), lambda i: (i, 0))
    return pl.pallas_call(
        _post_mlp_kernel,
        grid=(m // bm,),
        in_specs=[row, row, pl.BlockSpec((1, d), lambda i: (0, 0))],
        out_specs=row,
        out_shape=jax.ShapeDtypeStruct((m, d), F32),
        compiler_params=_params("parallel"),
        name="post_mlp_norm",
    )(x, y, g.reshape(1, d))


_NT_DIMS = (((1,), (1,)), ((), ()))


def _resident_rows_spec(bm, k):
    return pl.BlockSpec((bm, k), lambda i, j: (i, 0), pipeline_mode=pl.Buffered(1))


def _mm_nt_scale_kernel(a_ref, wt_ref, s_ref, o_ref):
    acc = lax.dot_general(a_ref[...], wt_ref[...].astype(BF16), _NT_DIMS, preferred_element_type=F32)
    o_ref[...] = (acc * s_ref[...]).astype(o_ref.dtype)


def _mm_nt_kernel(a_ref, wt_ref, o_ref):
    acc = lax.dot_general(a_ref[...], wt_ref[...].astype(BF16), _NT_DIMS, preferred_element_type=F32)
    o_ref[...] = acc.astype(o_ref.dtype)


def _matmul_nt(a, wt, row0, n, out_dtype, *, name, bm=MM_BM, bn=MM_BN, col_scale=None):
    m, k = a.shape
    in_specs = [_resident_rows_spec(bm, k),
                pl.BlockSpec((pl.Element(bn), pl.Element(k)),
                             lambda i, j: ((row0 // SUBLANES + j * (bn // SUBLANES)) * SUBLANES, 0))]
    args = [a, wt]
    body = _mm_nt_kernel
    if col_scale is not None:
        body = _mm_nt_scale_kernel
        in_specs.append(pl.BlockSpec((1, bn), lambda i, j: (0, j)))
        args.append(col_scale.reshape(1, n))
    return pl.pallas_call(
        body,
        grid=(m // bm, n // bn),
        in_specs=in_specs,
        out_specs=pl.BlockSpec((bm, bn), lambda i, j: (i, j)),
        out_shape=jax.ShapeDtypeStruct((m, n), out_dtype),
        compiler_params=_params("parallel", "arbitrary"),
        name=name,
    )(*args)


def _mm_plain_kernel(a_ref, w_ref, o_ref):
    o_ref[...] = jnp.dot(a_ref[...], w_ref[...].astype(BF16), preferred_element_type=F32).astype(o_ref.dtype)


def _mm_relu2_kernel(a_ref, w_ref, o_ref):
    acc = jnp.dot(a_ref[...], w_ref[...].astype(BF16), preferred_element_type=F32)
    r = jnp.maximum(acc, 0.0)
    o_ref[...] = (r * r).astype(o_ref.dtype)


def _matmul(a, w, out_dtype, *, name, bm=MM_BM, bn=MM_BN, relu2=False):
    m, k = a.shape
    n = w.shape[1]
    return pl.pallas_call(
        _mm_relu2_kernel if relu2 else _mm_plain_kernel,
        grid=(m // bm, n // bn),
        in_specs=[_resident_rows_spec(bm, k), pl.BlockSpec((k, bn), lambda i, j: (0, j))],
        out_specs=pl.BlockSpec((bm, bn), lambda i, j: (i, j)),
        out_shape=jax.ShapeDtypeStruct((m, n), out_dtype),
        compiler_params=_params("parallel", "arbitrary"),
        name=name,
    )(a, w)


def _mm_kacc_kernel(a_ref, w_ref, o_ref, acc_ref):
    kk = pl.program_id(2)

    @pl.when(kk == 0)
    def _():
        acc_ref[...] = jnp.zeros_like(acc_ref)

    acc_ref[...] += jnp.dot(a_ref[...], w_ref[...].astype(BF16), preferred_element_type=F32)

    @pl.when(kk == pl.num_programs(2) - 1)
    def _():
        o_ref[...] = acc_ref[...].astype(o_ref.dtype)


def _matmul_ktiled(a, w, out_dtype, *, bk, name, bm=MM_BM, bn=MM_BN):
    m, k = a.shape
    n = w.shape[1]
    return pl.pallas_call(
        _mm_kacc_kernel,
        grid=(m // bm, n // bn, k // bk),
        in_specs=[pl.BlockSpec((bm, bk), lambda i, j, kk: (i, kk)), pl.BlockSpec((bk, bn), lambda i, j, kk: (kk, j))],
        out_specs=pl.BlockSpec((bm, bn), lambda i, j, kk: (i, j)),
        out_shape=jax.ShapeDtypeStruct((m, n), out_dtype),
        scratch_shapes=[pltpu.VMEM((bm, bn), F32)],
        compiler_params=_params("parallel", "parallel", "arbitrary"),
        name=name,
    )(a, w)


def _merge_kernel(c_ref, a_ref, wc_ref, wa_ref, ga_ref, gb_ref, o_ref):
    yc = jnp.dot(c_ref[...], wc_ref[...].astype(BF16), preferred_element_type=F32)
    ya = jnp.dot(a_ref[...], wa_ref[...].astype(BF16), preferred_element_type=F32)
    ga = jax.nn.sigmoid(ga_ref[...].astype(F32))
    gb = jax.nn.sigmoid(gb_ref[...].astype(F32))
    o_ref[...] = (ga * yc + gb * ya).astype(o_ref.dtype)


def _merge(conv_act, attn_act, wc, wa, gates_ab, *, bm=MM_BM // 2, bn=MM_BN):
    m, kc = conv_act.shape
    ka = attn_act.shape[1]
    n = wc.shape[1]
    jb = n // bn
    return pl.pallas_call(
        _merge_kernel,
        grid=(m // bm, n // bn),
        in_specs=[
            pl.BlockSpec((bm, kc), lambda i, j: (i, 0)),
            pl.BlockSpec((bm, ka), lambda i, j: (i, 0)),
            pl.BlockSpec((kc, bn), lambda i, j: (0, j)),
            pl.BlockSpec((ka, bn), lambda i, j: (0, j)),
            pl.BlockSpec((bm, bn), lambda i, j: (i, j)),
            pl.BlockSpec((bm, bn), lambda i, j: (i, jb + j)),
        ],
        out_specs=pl.BlockSpec((bm, bn), lambda i, j: (i, j)),
        out_shape=jax.ShapeDtypeStruct((m, n), BF16),
        compiler_params=_params("parallel", "arbitrary"),
        name="gated_merge",
    )(conv_act, attn_act, wc, wa, gates_ab, gates_ab)


def _gelu_tanh(x):
    return 0.5 * x * (1.0 + jnp.tanh(np.sqrt(2.0 / np.pi) * (x + 0.044715 * (x * x * x))))


def _compress_kernel(xk_ref, xv_ref, pk_ref, pv_ref, w1k_ref, w1v_ref, w2k_ref, w2v_ref, ok_ref, ov_ref,
                     pair_ref, flat_ref):
    for x_ref, p_ref, w1_ref, w2_ref, o_ref in ((xk_ref, pk_ref, w1k_ref, w2k_ref, ok_ref),
                                                (xv_ref, pv_ref, w1v_ref, w2v_ref, ov_ref)):
        pair_ref[...] = pltpu.bitcast(x_ref[0], jnp.uint32)
        for l in range(CMP_STRIDE):
            word = pair_ref[pl.ds(l // 2, N_CHUNK, stride=CMP_STRIDE // 2), :]
            bits = (word << 16) if l % 2 == 0 else (word & jnp.uint32(0xFFFF0000))
            flat_ref[:, l * HEAD_DIM:(l + 1) * HEAD_DIM] = pltpu.bitcast(bits, F32)
        x = flat_ref[...]
        first = jnp.dot((x + p_ref[0:1, :]).astype(BF16), w1_ref[0].astype(BF16), preferred_element_type=F32)
        second = jnp.dot((x + p_ref[1:2, :]).astype(BF16), w1_ref[1].astype(BF16), preferred_element_type=F32)
        h = first + pltpu.roll(second, N_CHUNK - 1, axis=0)
        o_ref[0, 0] = jnp.dot(_gelu_tanh(h).astype(BF16), w2_ref[...].astype(BF16),
                              preferred_element_type=F32).astype(o_ref.dtype)


def _compress(proj3, pk, pv, w1k, w1v, w2k, w2v):
    b = proj3.shape[0]
    half = CMP_STRIDE * HEAD_DIM
    xspec = lambda off: pl.BlockSpec((1, SEQ, HEAD_DIM), lambda i, g: (i, 0, off // HEAD_DIM + g))
    pspec = pl.BlockSpec((2, half), lambda i, g: (0, 0))
    w1spec = pl.BlockSpec((2, half, CMP_HIDDEN), lambda i, g: (0, 0, 0))
    w2spec = pl.BlockSpec((CMP_HIDDEN, HEAD_DIM), lambda i, g: (0, 0))
    ospec = pl.BlockSpec((1, 1, N_CHUNK, HEAD_DIM), lambda i, g: (i, g, 0, 0))
    oshape = jax.ShapeDtypeStruct((b, N_KV_GROUPS, N_CHUNK, HEAD_DIM), BF16)
    return pl.pallas_call(
        _compress_kernel,
        grid=(b, N_KV_GROUPS),
        in_specs=[xspec(OFF_KC), xspec(OFF_VC), pspec, pspec, w1spec, w1spec, w2spec, w2spec],
        out_specs=[ospec, ospec],
        out_shape=[oshape, oshape],
        scratch_shapes=[pltpu.VMEM((SEQ // 2, HEAD_DIM), jnp.uint32), pltpu.VMEM((N_CHUNK, half), F32)],
        compiler_params=_params("parallel", "parallel"),
        name="compress_kv",
    )(proj3, proj3, pk, pv, w1k, w1v, w2k, w2v)


def _attn_kernel(q_ref, ks_ref, vs_ref, kw_ref, vw_ref, kc_ref, vc_ref, gl_ref, ovl_ref, blk_ref, o_ref,
                 s_ref, mx_ref, acc_ref, sw_ref, ow_ref, *, tq):
    tk = tq
    grp = pl.program_id(1)
    qi = pl.program_id(2)
    q0 = qi * tq
    nh = HEADS_PER_GROUP
    rows = nh * tq
    nt = (((1,), (1,)), ((), ()))

    q = q_ref[0]
    qs = jnp.concatenate([q[:, r * HEAD_DIM:(r + 1) * HEAD_DIM] for r in range(nh)], axis=0)

    n_win = WINDOW // tk + 1
    wk0 = pl.multiple_of(jnp.maximum(qi - (n_win - 1), 0) * tk, tk)
    sc = lax.dot_general(qs, kw_ref[0, pl.ds(wk0, n_win * tk), :], nt, preferred_element_type=F32)
    key = wk0 + lax.broadcasted_iota(jnp.int32, (tq, n_win * tk), 1)
    t_abs = q0 + lax.broadcasted_iota(jnp.int32, (tq, n_win * tk), 0)
    win_bias = jnp.where((key <= t_abs) & (key > t_abs - WINDOW), 0.0, NEG_INF)
    m_win = None
    for i in range(n_win):
        part = sc[:, i * tk:(i + 1) * tk].reshape(nh, tq, tk) + win_bias[None, :, i * tk:(i + 1) * tk]
        part = part.reshape(rows, tk)
        sw_ref[i] = part
        for c in range(tk // LANES):
            piece = part[:, c * LANES:(c + 1) * LANES]
            m_win = piece if m_win is None else jnp.maximum(m_win, piece)
    n_pad = jnp.maximum(WINDOW - 1 - (q0 + lax.broadcasted_iota(jnp.int32, (tq, LANES), 0)), 0).astype(F32)
    n_pad = jnp.concatenate([n_pad] * nh, axis=0)
    m_win = jnp.broadcast_to(jnp.max(m_win, axis=-1, keepdims=True), (rows, LANES))
    m_win = jnp.where(n_pad > 0.0, jnp.maximum(m_win, 0.0), m_win)
    m_wide = jnp.concatenate([m_win] * (tk // LANES), axis=1)
    pr = jnp.concatenate([jnp.exp2(sw_ref[i] - m_wide).astype(BF16) for i in range(n_win)], axis=1)
    v_ext = jnp.concatenate([vw_ref[0, pl.ds(wk0, n_win * tk), :], jnp.ones((n_win * tk, LANES), BF16)], axis=1)
    acc_w = jnp.dot(pr, v_ext, preferred_element_type=F32)
    pad_term = n_pad * jnp.exp2(jnp.where(n_pad > 0.0, -m_win, 0.0))
    ow_ref[...] = acc_w[:, :HEAD_DIM] * (1.0 / (acc_w[:, HEAD_DIM:] + pad_term))

    s = lax.dot_general(qs, kc_ref[0, 0], nt, preferred_element_type=F32).reshape(nh, tq, N_CHUNK)
    n_idx = lax.broadcasted_iota(jnp.int32, (tq, N_CHUNK), 1)
    t_idx = q0 + lax.broadcasted_iota(jnp.int32, (tq, N_CHUNK), 0)
    vis = ((n_idx * CMP_STRIDE + (CMP_BLOCK - 1) <= t_idx) & (n_idx < N_CMP))[None]
    sm = jnp.where(vis, s, NEG_INF)
    mx = jnp.maximum(jnp.max(sm, axis=-1, keepdims=True), 0.5 * NEG_INF)
    e = jnp.exp2(sm - mx)
    p = e * (1.0 / jnp.maximum(jnp.sum(e, axis=-1, keepdims=True), 1e-30))
    o_cmp = jnp.dot(p.reshape(rows, N_CHUNK).astype(BF16), vc_ref[0, 0], preferred_element_type=F32)

    p_sum = p[0] + p[1] + p[2] + p[3]
    p_hi = p_sum.astype(BF16)
    rem = p_sum - p_hi.astype(F32)
    p_mid = rem.astype(BF16)
    p_lo = (rem - p_mid.astype(F32)).astype(BF16)
    ovl = ovl_ref[...]
    imp = (lax.dot_general(ovl, p_hi, nt, preferred_element_type=F32)
           + lax.dot_general(ovl, p_mid, nt, preferred_element_type=F32)
           + lax.dot_general(ovl, p_lo, nt, preferred_element_type=F32))

    j_idx = lax.broadcasted_iota(jnp.int32, (N_SLC, tq), 0)
    cur = jnp.right_shift(q0 + lax.broadcasted_iota(jnp.int32, (N_SLC, tq), 1), int(np.log2(SLC_BLOCK)))
    valid = j_idx <= cur
    forced = (j_idx == 0) | (j_idx == cur) | (j_idx == cur - 1)
    score = jnp.where(forced & valid, FORCED_SCORE, jnp.where(valid, imp, MASKED_SCORE))
    rank = jnp.zeros((N_SLC, tq), jnp.int32)
    for k in range(N_SLC):
        sk = score[k:k + 1, :]
        beats = (sk > score) | ((sk == score) & (j_idx > k))
        rank = rank + beats.astype(jnp.int32)
    unsel_t = jnp.where(rank < N_SELECT, 0.0, 1.0)
    unsel_t = jnp.concatenate([unsel_t, jnp.zeros((LANES - N_SLC, tq), F32)], axis=0)
    unsel = jnp.transpose(unsel_t).astype(BF16)

    q_slc = jnp.concatenate([qs, jnp.concatenate([unsel] * nh, axis=0)], axis=1)

    local_q = lax.broadcasted_iota(jnp.int32, (tq, tk), 0)
    local_k = lax.broadcasted_iota(jnp.int32, (tq, tk), 1)
    causal_bias = jnp.where(local_k <= local_q, 0.0, NEG_INF)

    def slc_pass1(n):
        k = jnp.concatenate([ks_ref[0, 0:n * tk, :], blk_ref[0:n * tk, :]], axis=1)
        sc = lax.dot_general(q_slc, k, nt, preferred_element_type=F32)
        m = None
        for i in range(n):
            part = sc[:, i * tk:(i + 1) * tk]
            if i == n - 1:
                part = (part.reshape(nh, tq, tk) + causal_bias[None]).reshape(rows, tk)
            s_ref[i] = part
            for c in range(tk // LANES):
                piece = part[:, c * LANES:(c + 1) * LANES]
                m = piece if m is None else jnp.maximum(m, piece)
        mx_ref[...] = jnp.broadcast_to(jnp.max(m, axis=-1, keepdims=True), (rows, LANES))

    def slc_pass2(n):
        v_ext = jnp.concatenate([vs_ref[0, 0:n * tk, :], jnp.ones((n * tk, LANES), BF16)], axis=1)
        m_wide = jnp.concatenate([mx_ref[...]] * (tk // LANES), axis=1)
        pr = jnp.concatenate([jnp.exp2(s_ref[i] - m_wide).astype(BF16) for i in range(n)], axis=1)
        acc_ref[...] = jnp.dot(pr, v_ext, preferred_element_type=F32)

    n_q_tiles = SEQ // tq
    for n_tiles in range(1, n_q_tiles + 1):
        pl.when(qi == n_tiles - 1)(functools.partial(slc_pass1, n_tiles))

    for n_tiles in range(1, n_q_tiles + 1):
        pl.when(qi == n_tiles - 1)(functools.partial(slc_pass2, n_tiles))

    acc = acc_ref[...]
    o_slc = acc[:, :HEAD_DIM] * (1.0 / acc[:, HEAD_DIM:])

    o_win = ow_ref[...]

    gates = jax.nn.sigmoid(gl_ref[0])
    gates = pltpu.roll(gates, (grp * (LANES - nh * N_NSA_BRANCHES)) % LANES, axis=1)
    for r in range(nh):
        sl = slice(r * tq, (r + 1) * tq)
        c = r * N_NSA_BRANCHES
        o_r = (gates[:, c:c + 1] * o_cmp[sl] + gates[:, c + 1:c + 2] * o_slc[sl] + gates[:, c + 2:c + 3] * o_win[sl])
        o_ref[0, :, r * HEAD_DIM:(r + 1) * HEAD_DIM] = o_r.astype(o_ref.dtype)


def _attention(proj3, gate_logits3, k_cmp, v_cmp, overlap_t, block_cols, *, tq=ATTN_TILE):
    b = proj3.shape[0]
    gw = HEADS_PER_GROUP * HEAD_DIM
    kv = lambda off: pl.BlockSpec((1, SEQ, HEAD_DIM), lambda i, g, t: (i, 0, off // HEAD_DIM + g))
    cmp_spec = pl.BlockSpec((1, 1, N_CHUNK, HEAD_DIM), lambda i, g, t: (i, g, 0, 0))
    rows = HEADS_PER_GROUP * tq
    return pl.pallas_call(
        functools.partial(_attn_kernel, tq=tq),
        grid=(b, N_KV_GROUPS, SEQ // tq),
        in_specs=[
            pl.BlockSpec((1, tq, gw), lambda i, g, t: (i, t, OFF_Q // gw + g)),
            kv(OFF_KS), kv(OFF_VS), kv(OFF_KW), kv(OFF_VW),
            cmp_spec, cmp_spec,
            pl.BlockSpec((1, tq, LANES), lambda i, g, t: (i, t, 0)),
            pl.BlockSpec((N_SLC, N_CHUNK), lambda i, g, t: (0, 0)),
            pl.BlockSpec((SEQ, LANES), lambda i, g, t: (0, 0)),
        ],
        out_specs=pl.BlockSpec((1, tq, gw), lambda i, g, t: (i, t, g)),
        out_shape=jax.ShapeDtypeStruct((b, SEQ, D_Q), BF16),
        scratch_shapes=[
            pltpu.VMEM((SEQ // tq, rows, tq), F32),
            pltpu.VMEM((rows, LANES), F32),
            pltpu.VMEM((rows, 2 * HEAD_DIM), F32),
            pltpu.VMEM((WINDOW // tq + 1, rows, tq), F32),
            pltpu.VMEM((rows, HEAD_DIM), F32),
        ],
        compiler_params=_params("parallel", "parallel", "arbitrary"),
        name="nsa_attention",
    )(proj3, proj3, proj3, proj3, proj3, k_cmp, v_cmp, gate_logits3, overlap_t, block_cols)


def _conv_kernel(a_ref, b_ref, ah_ref, bh_ref, w_ref, bias_ref, lg_ref, lb_ref, o_ref, u_ref, y_ref, *, ts):
    nc = D_CONV // LANES
    first = pl.program_id(1) == 0
    u_main = a_ref[0].astype(F32) * jax.nn.sigmoid(b_ref[0].astype(F32))
    u_halo = ah_ref[0].astype(F32) * jax.nn.sigmoid(bh_ref[0].astype(F32))
    u_halo = jnp.where(first, 0.0, u_halo)
    for c in range(nc):
        u_ref[c, 0:HALO, :] = u_halo[:, c * LANES:(c + 1) * LANES]
        u_ref[c, HALO:HALO + ts, :] = u_main[:, c * LANES:(c + 1) * LANES]

    rc = 64
    base = HALO - (CONV_WIDTH - 1)

    def chunk_body(c, carry):
        for r0 in range(0, ts, rc):
            acc = jnp.zeros((rc, LANES), F32)
            for j in range(CONV_WIDTH):
                acc = acc + u_ref[c, pl.ds(base + r0 + j, rc), :] * w_ref[c, j:j + 1, :]
            y_ref[c, r0:r0 + rc, :] = acc
        return carry

    lax.fori_loop(0, nc, chunk_body, 0)

    y = jnp.concatenate([y_ref[c] for c in range(nc)], axis=1) + bias_ref[...]
    mu = jnp.mean(y, axis=-1, keepdims=True)
    d = y - mu
    var = jnp.mean(d * d, axis=-1, keepdims=True)
    z = d * lax.rsqrt(var + LN_EPS) * lg_ref[...] + lb_ref[...]
    o_ref[0] = (z * jax.nn.sigmoid(z)).astype(o_ref.dtype)


def _conformer_conv(proj3, w_chunks, b_dw, ln_g, ln_b, *, ts=256):
    b = proj3.shape[0]
    nc = D_CONV // LANES
    per = ts // HALO
    main = lambda off: pl.BlockSpec((1, ts, D_CONV), lambda i, t: (i, t, off // D_CONV))
    halo = lambda off: pl.BlockSpec((1, HALO, D_CONV), lambda i, t: (i, jnp.maximum(t * per - 1, 0), off // D_CONV))
    vec = pl.BlockSpec((1, D_CONV), lambda i, t: (0, 0))
    return pl.pallas_call(
        functools.partial(_conv_kernel, ts=ts),
        grid=(b, SEQ // ts),
        in_specs=[main(OFF_GLU_A), main(OFF_GLU_B), halo(OFF_GLU_A), halo(OFF_GLU_B),
                  pl.BlockSpec((nc, HALO, LANES), lambda i, t: (0, 0, 0)), vec, vec, vec],
        out_specs=pl.BlockSpec((1, ts, D_CONV), lambda i, t: (i, t, 0)),
        out_shape=jax.ShapeDtypeStruct((b, SEQ, D_CONV), BF16),
        scratch_shapes=[pltpu.VMEM((nc, HALO + ts, LANES), F32), pltpu.VMEM((nc, ts, LANES), F32)],
        compiler_params=_params("parallel", "arbitrary"),
        name="conformer_conv",
    )(proj3, proj3, proj3, proj3, w_chunks, b_dw.reshape(1, D_CONV), ln_g.reshape(1, D_CONV), ln_b.reshape(1, D_CONV))


def _overlap_t():
    cmp_start = np.arange(N_CHUNK) * CMP_STRIDE
    slc_start = np.arange(N_SLC) * SLC_BLOCK
    ov = ((cmp_start[None, :] < slc_start[:, None] + SLC_BLOCK) & (cmp_start[None, :] + CMP_BLOCK > slc_start[:, None])
          & (np.arange(N_CHUNK)[None, :] < N_CMP))
    return jnp.asarray(ov.astype(np.float32), dtype=BF16)


def _block_cols():
    own = (np.arange(SEQ)[:, None] // SLC_BLOCK) == np.arange(LANES)[None, :]
    return jnp.asarray(np.where(own, -MASK_BIG, 0.0).astype(np.float32), dtype=BF16)


def _layer(x, norm_mix_pre, w_in, pos_cmp_k, w_cmp_k1, w_cmp_k2, pos_cmp_v, w_cmp_v1, w_cmp_v2,
           w_dw, b_dw, ln_conv_g, ln_conv_b, w_conv_out, w_attn_out, w_out, norm_mix_post,
           norm_mlp_pre, w_up, w_down, norm_mlp_post):
    b, s, d = x.shape
    m = b * s
    x2 = x.reshape(m, d)
    half = CMP_STRIDE * HEAD_DIM

    w_in_t = jnp.swapaxes(w_in, 0, 1)
    col_scale = jnp.ones((D_MAIN,), F32).at[OFF_Q:OFF_Q + D_Q].set(HEAD_DIM ** -0.5 * LOG2_E)

    u, gate_logits = _rmsnorm_and_branch_gates(x2, norm_mix_pre, w_in_t, D_MAIN)
    proj = _matmul_nt(u, w_in_t, 0, D_MAIN, BF16, name="in_proj", col_scale=col_scale)
    proj3 = proj.reshape(b, s, D_MAIN)

    w_chunks = jnp.pad(w_dw.reshape(CONV_WIDTH, D_CONV), ((0, HALO - CONV_WIDTH), (0, 0)))
    w_chunks = w_chunks.reshape(HALO, D_CONV // LANES, LANES).transpose(1, 0, 2)
    gates_ab = _matmul_nt(u, w_in_t, D_MAIN + N_GATE, 2 * D_MODEL, BF16, name="in_proj_merge_gates")
    conv = _conformer_conv(proj3, w_chunks, b_dw, ln_conv_g, ln_conv_b)

    k_cmp, v_cmp = _compress(
        proj3, pos_cmp_k.reshape(2, half), pos_cmp_v.reshape(2, half),
        w_cmp_k1.reshape(2, half, CMP_HIDDEN), w_cmp_v1.reshape(2, half, CMP_HIDDEN), w_cmp_k2, w_cmp_v2)

    attn = _attention(proj3, gate_logits.reshape(b, s, LANES), k_cmp, v_cmp, _overlap_t(), _block_cols())

    merged = _merge(conv.reshape(m, D_CONV), attn.reshape(m, D_Q), w_conv_out, w_attn_out, gates_ab)
    z = _matmul(merged, w_out, BF16, name="out_proj")
    x1, h = _post_mix(x2, z, norm_mix_post, norm_mlp_pre)

    hidden = _matmul(h, w_up, BF16, name="mlp_up", relu2=True)
    y = _matmul_ktiled(hidden, w_down, BF16, bm=2048, bn=1024, bk=1024, name="mlp_down")
    return _post_mlp(x1, y, norm_mlp_post).reshape(b, s, d)


def kernel(x, norm_mix_pre, w_in, pos_cmp_k, w_cmp_k1, w_cmp_k2, pos_cmp_v, w_cmp_v1, w_cmp_v2, w_dw, b_dw,
           ln_conv_g, ln_conv_b, w_conv_out, w_attn_out, w_out, norm_mix_post, norm_mlp_pre, w_up, w_down,
           norm_mlp_post):
    for l in range(norm_mix_pre.shape[0]):
        x = _layer(x, norm_mix_pre[l], w_in[l], pos_cmp_k[l], w_cmp_k1[l], w_cmp_k2[l], pos_cmp_v[l], w_cmp_v1[l],
                   w_cmp_v2[l], w_dw[l], b_dw[l], ln_conv_g[l], ln_conv_b[l], w_conv_out[l], w_attn_out[l], w_out[l],
                   norm_mix_post[l], norm_mlp_pre[l], w_up[l], w_down[l], norm_mlp_post[l])
    return x
```

```python
import functools

import numpy as np
import jax
import jax.numpy as jnp
from jax import lax
from jax.experimental import pallas as pl
from jax.experimental.pallas import tpu as pltpu

D_MODEL = 4096
SEQ = 2048
D_CONV = D_MODEL // 2
CONV_WIDTH = 31
HEAD_DIM = 128
N_HEADS = 16
N_KV_GROUPS = 4
HEADS_PER_GROUP = N_HEADS // N_KV_GROUPS
CMP_BLOCK = 32
CMP_STRIDE = 16
CMP_HIDDEN = 2 * HEAD_DIM
SLC_BLOCK = 64
N_SELECT = 16
WINDOW = 512
N_NSA_BRANCHES = 3
D_Q = N_HEADS * HEAD_DIM
D_KV = N_KV_GROUPS * HEAD_DIM
N_GATE = N_NSA_BRANCHES * N_HEADS
D_MAIN = 2 * D_CONV + D_Q + 6 * D_KV
N_CHUNK = SEQ // CMP_STRIDE
N_CMP = N_CHUNK - CMP_BLOCK // CMP_STRIDE + 1
N_SLC = SEQ // SLC_BLOCK

NORM_EPS = 1e-6
LN_EPS = 1e-5
FORCED_SCORE = 1e4
MASKED_SCORE = -1e4
NEG_INF = -1e30
MASK_BIG = 2.0 ** 100
LOG2_E = float(np.log2(np.e))

LANES = 128
SUBLANES = 8
HALO = 32
ATTN_TILE = 256
MM_BM = 2048
MM_BN = 512
VMEM_LIMIT = 56 * 1024 * 1024

OFF_GLU_A = 0
OFF_GLU_B = D_CONV
OFF_Q = 2 * D_CONV
OFF_KC = OFF_Q + D_Q
OFF_VC = OFF_KC + D_KV
OFF_KS = OFF_VC + D_KV
OFF_VS = OFF_KS + D_KV
OFF_KW = OFF_VS + D_KV
OFF_VW = OFF_KW + D_KV

F32 = jnp.float32
BF16 = jnp.bfloat16


def _params(*sem):
    return pltpu.CompilerParams(dimension_semantics=sem, vmem_limit_bytes=VMEM_LIMIT)


def _rmsnorm_gates_kernel(x_ref, g_ref, wt_ref, u_ref, o_ref):
    x = x_ref[...]
    y = x * lax.rsqrt(jnp.mean(x * x, axis=-1, keepdims=True) + NORM_EPS)
    u = (y * g_ref[...]).astype(u_ref.dtype)
    u_ref[...] = u
    o_ref[...] = lax.dot_general(u, wt_ref[...].astype(BF16), (((1,), (1,)), ((), ())), preferred_element_type=F32)


def _rmsnorm_and_branch_gates(x, g, wt, row0, bm=512):
    m, d = x.shape
    return pl.pallas_call(
        _rmsnorm_gates_kernel,
        grid=(m // bm,),
        in_specs=[pl.BlockSpec((bm, d), lambda i: (i, 0)), pl.BlockSpec((1, d), lambda i: (0, 0)),
                  pl.BlockSpec((pl.Element(LANES), pl.Element(d)), lambda i: (row0, 0))],
        out_specs=[pl.BlockSpec((bm, d), lambda i: (i, 0)), pl.BlockSpec((bm, LANES), lambda i: (i, 0))],
        out_shape=[jax.ShapeDtypeStruct((m, d), BF16), jax.ShapeDtypeStruct((m, LANES), F32)],
        compiler_params=_params("parallel"),
        name="rmsnorm_in_branch_gates",
    )(x, g.reshape(1, d), wt)


def _post_mix_kernel(x_ref, z_ref, g1_ref, g2_ref, x1_ref, h_ref):
    z = z_ref[...].astype(F32)
    zn = z * lax.rsqrt(jnp.mean(z * z, axis=-1, keepdims=True) + NORM_EPS)
    x1 = x_ref[...] + zn * g1_ref[...]
    x1_ref[...] = x1
    hn = x1 * lax.rsqrt(jnp.mean(x1 * x1, axis=-1, keepdims=True) + NORM_EPS)
    h_ref[...] = (hn * g2_ref[...]).astype(h_ref.dtype)


def _post_mix(x, z, g1, g2, bm=256):
    m, d = x.shape
    row = pl.BlockSpec((bm, d), lambda i: (i, 0))
    vec = pl.BlockSpec((1, d), lambda i: (0, 0))
    return pl.pallas_call(
        _post_mix_kernel,
        grid=(m // bm,),
        in_specs=[row, row, vec, vec],
        out_specs=[row, row],
        out_shape=[jax.ShapeDtypeStruct((m, d), F32), jax.ShapeDtypeStruct((m, d), BF16)],
        compiler_params=_params("parallel"),
        name="post_mix_norm",
    )(x, z, g1.reshape(1, d), g2.reshape(1, d))


def _post_mlp_kernel(x_ref, y_ref, g_ref, o_ref):
    y = y_ref[...].astype(F32)
    yn = y * lax.rsqrt(jnp.mean(y * y, axis=-1, keepdims=True) + NORM_EPS)
    o_ref[...] = x_ref[...] + yn * g_ref[...]


def _post_mlp(x, y, g, bm=512):
    m, d = x.shape
    row = pl.BlockSpec((bm, d), lambda i: (i, 0))
    return pl.pallas_call(
        _post_mlp_kernel,
        grid=(m // bm,),
        in_specs=[row, row, pl.BlockSpec((1, d), lambda i: (0, 0))],
        out_specs=row,
        out_shape=jax.ShapeDtypeStruct((m, d), F32),
        compiler_params=_params("parallel"),
        name="post_mlp_norm",
    )(x, y, g.reshape(1, d))


_NT_DIMS = (((1,), (1,)), ((), ()))


def _resident_rows_spec(bm, k):
    return pl.BlockSpec((bm, k), lambda i, j: (i, 0), pipeline_mode=pl.Buffered(1))


def _mm_nt_scale_kernel(a_ref, wt_ref, s_ref, o_ref):
    acc = lax.dot_general(a_ref[...], wt_ref[...].astype(BF16), _NT_DIMS, preferred_element_type=F32)
    o_ref[...] = (acc * s_ref[...]).astype(o_ref.dtype)


def _mm_nt_kernel(a_ref, wt_ref, o_ref):
    acc = lax.dot_general(a_ref[...], wt_ref[...].astype(BF16), _NT_DIMS, preferred_element_type=F32)
    o_ref[...] = acc.astype(o_ref.dtype)


def _matmul_nt(a, wt, row0, n, out_dtype, *, name, bm=MM_BM, bn=MM_BN, col_scale=None):
    m, k = a.shape
    in_specs = [_resident_rows_spec(bm, k),
                pl.BlockSpec((pl.Element(bn), pl.Element(k)),
                             lambda i, j: ((row0 // SUBLANES + j * (bn // SUBLANES)) * SUBLANES, 0))]
    args = [a, wt]
    body = _mm_nt_kernel
    if col_scale is not None:
        body = _mm_nt_scale_kernel
        in_specs.append(pl.BlockSpec((1, bn), lambda i, j: (0, j)))
        args.append(col_scale.reshape(1, n))
    return pl.pallas_call(
        body,
        grid=(m // bm, n // bn),
        in_specs=in_specs,
        out_specs=pl.BlockSpec((bm, bn), lambda i, j: (i, j)),
        out_shape=jax.ShapeDtypeStruct((m, n), out_dtype),
        compiler_params=_params("parallel", "arbitrary"),
        name=name,
    )(*args)


def _mm_plain_kernel(a_ref, w_ref, o_ref):
    o_ref[...] = jnp.dot(a_ref[...], w_ref[...].astype(BF16), preferred_element_type=F32).astype(o_ref.dtype)


def _mm_relu2_kernel(a_ref, w_ref, o_ref):
    acc = jnp.dot(a_ref[...], w_ref[...].astype(BF16), preferred_element_type=F32)
    r = jnp.maximum(acc, 0.0)
    o_ref[...] = (r * r).astype(o_ref.dtype)


def _matmul(a, w, out_dtype, *, name, bm=MM_BM, bn=MM_BN, relu2=False):
    m, k = a.shape
    n = w.shape[1]
    return pl.pallas_call(
        _mm_relu2_kernel if relu2 else _mm_plain_kernel,
        grid=(m // bm, n // bn),
        in_specs=[_resident_rows_spec(bm, k), pl.BlockSpec((k, bn), lambda i, j: (0, j))],
        out_specs=pl.BlockSpec((bm, bn), lambda i, j: (i, j)),
        out_shape=jax.ShapeDtypeStruct((m, n), out_dtype),
        compiler_params=_params("parallel", "arbitrary"),
        name=name,
    )(a, w)


def _mm_kacc_kernel(a_ref, w_ref, o_ref, acc_ref):
    kk = pl.program_id(2)

    @pl.when(kk == 0)
    def _():
        acc_ref[...] = jnp.zeros_like(acc_ref)

    acc_ref[...] += jnp.dot(a_ref[...], w_ref[...].astype(BF16), preferred_element_type=F32)

    @pl.when(kk == pl.num_programs(2) - 1)
    def _():
        o_ref[...] = acc_ref[...].astype(o_ref.dtype)


def _matmul_ktiled(a, w, out_dtype, *, bk, name, bm=MM_BM, bn=MM_BN):
    m, k = a.shape
    n = w.shape[1]
    return pl.pallas_call(
        _mm_kacc_kernel,
        grid=(m // bm, n // bn, k // bk),
        in_specs=[pl.BlockSpec((bm, bk), lambda i, j, kk: (i, kk)), pl.BlockSpec((bk, bn), lambda i, j, kk: (kk, j))],
        out_specs=pl.BlockSpec((bm, bn), lambda i, j, kk: (i, j)),
        out_shape=jax.ShapeDtypeStruct((m, n), out_dtype),
        scratch_shapes=[pltpu.VMEM((bm, bn), F32)],
        compiler_params=_params("parallel", "parallel", "arbitrary"),
        name=name,
    )(a, w)


def _merge_kernel(c_ref, a_ref, wc_ref, wa_ref, ga_ref, gb_ref, o_ref):
    yc = jnp.dot(c_ref[...], wc_ref[...].astype(BF16), preferred_element_type=F32)
    ya = jnp.dot(a_ref[...], wa_ref[...].astype(BF16), preferred_element_type=F32)
    ga = jax.nn.sigmoid(ga_ref[...].astype(F32))
    gb = jax.nn.sigmoid(gb_ref[...].astype(F32))
    o_ref[...] = (ga * yc + gb * ya).astype(o_ref.dtype)


def _merge(conv_act, attn_act, wc, wa, gates_ab, *, bm=MM_BM // 2, bn=MM_BN):
    m, kc = conv_act.shape
    ka = attn_act.shape[1]
    n = wc.shape[1]
    jb = n // bn
    return pl.pallas_call(
        _merge_kernel,
        grid=(m // bm, n // bn),
        in_specs=[
            pl.BlockSpec((bm, kc), lambda i, j: (i, 0)),
            pl.BlockSpec((bm, ka), lambda i, j: (i, 0)),
            pl.BlockSpec((kc, bn), lambda i, j: (0, j)),
            pl.BlockSpec((ka, bn), lambda i, j: (0, j)),
            pl.BlockSpec((bm, bn), lambda i, j: (i, j)),
            pl.BlockSpec((bm, bn), lambda i, j: (i, jb + j)),
        ],
        out_specs=pl.BlockSpec((bm, bn), lambda i, j: (i, j)),
        out_shape=jax.ShapeDtypeStruct((m, n), BF16),
        compiler_params=_params("parallel", "arbitrary"),
        name="gated_merge",
    )(conv_act, attn_act, wc, wa, gates_ab, gates_ab)


def _gelu_tanh(x):
    return 0.5 * x * (1.0 + jnp.tanh(np.sqrt(2.0 / np.pi) * (x + 0.044715 * (x * x * x))))


def _compress_kernel(xk_ref, xv_ref, pk_ref, pv_ref, w1k_ref, w1v_ref, w2k_ref, w2v_ref, ok_ref, ov_ref,
                     pair_ref, flat_ref):
    for x_ref, p_ref, w1_ref, w2_ref, o_ref in ((xk_ref, pk_ref, w1k_ref, w2k_ref, ok_ref),
                                                (xv_ref, pv_ref, w1v_ref, w2v_ref, ov_ref)):
        pair_ref[...] = pltpu.bitcast(x_ref[0], jnp.uint32)
        for l in range(CMP_STRIDE):
            word = pair_ref[pl.ds(l // 2, N_CHUNK, stride=CMP_STRIDE // 2), :]
            bits = (word << 16) if l % 2 == 0 else (word & jnp.uint32(0xFFFF0000))
            flat_ref[:, l * HEAD_DIM:(l + 1) * HEAD_DIM] = pltpu.bitcast(bits, F32)
        x = flat_ref[...]
        first = jnp.dot((x + p_ref[0:1, :]).astype(BF16), w1_ref[0].astype(BF16), preferred_element_type=F32)
        second = jnp.dot((x + p_ref[1:2, :]).astype(BF16), w1_ref[1].astype(BF16), preferred_element_type=F32)
        h = first + pltpu.roll(second, N_CHUNK - 1, axis=0)
        o_ref[0, 0] = jnp.dot(_gelu_tanh(h).astype(BF16), w2_ref[...].astype(BF16),
                              preferred_element_type=F32).astype(o_ref.dtype)


def _compress(proj3, pk, pv, w1k, w1v, w2k, w2v):
    b = proj3.shape[0]
    half = CMP_STRIDE * HEAD_DIM
    xspec = lambda off: pl.BlockSpec((1, SEQ, HEAD_DIM), lambda i, g: (i, 0, off // HEAD_DIM + g))
    pspec = pl.BlockSpec((2, half), lambda i, g: (0, 0))
    w1spec = pl.BlockSpec((2, half, CMP_HIDDEN), lambda i, g: (0, 0, 0))
    w2spec = pl.BlockSpec((CMP_HIDDEN, HEAD_DIM), lambda i, g: (0, 0))
    ospec = pl.BlockSpec((1, 1, N_CHUNK, HEAD_DIM), lambda i, g: (i, g, 0, 0))
    oshape = jax.ShapeDtypeStruct((b, N_KV_GROUPS, N_CHUNK, HEAD_DIM), BF16)
    return pl.pallas_call(
        _compress_kernel,
        grid=(b, N_KV_GROUPS),
        in_specs=[xspec(OFF_KC), xspec(OFF_VC), pspec, pspec, w1spec, w1spec, w2spec, w2spec],
        out_specs=[ospec, ospec],
        out_shape=[oshape, oshape],
        scratch_shapes=[pltpu.VMEM((SEQ // 2, HEAD_DIM), jnp.uint32), pltpu.VMEM((N_CHUNK, half), F32)],
        compiler_params=_params("parallel", "parallel"),
        name="compress_kv",
    )(proj3, proj3, pk, pv, w1k, w1v, w2k, w2v)


def _attn_kernel(q_ref, ks_ref, vs_ref, kw_ref, vw_ref, kc_ref, vc_ref, gl_ref, ovl_ref, blk_ref, o_ref,
                 s_ref, mx_ref, acc_ref, sw_ref, ow_ref, g_ref, *, tq):
    tk = tq
    grp = pl.program_id(1)
    qi = pl.program_id(2)
    q0 = qi * tq
    nh = HEADS_PER_GROUP
    rows = nh * tq
    nt = (((1,), (1,)), ((), ()))

    q = q_ref[0]
    qs = jnp.concatenate([q[:, r * HEAD_DIM:(r + 1) * HEAD_DIM] for r in range(nh)], axis=0)

    gates = jax.nn.sigmoid(gl_ref[0])
    gates = pltpu.roll(gates, (grp * (LANES - nh * N_NSA_BRANCHES)) % LANES, axis=1)
    for c in range(nh * N_NSA_BRANCHES):
        g_ref[c] = jnp.broadcast_to(gates[:, c:c + 1], (tq, LANES))

    n_win = WINDOW // tk + 1
    wk0 = pl.multiple_of(jnp.maximum(qi - (n_win - 1), 0) * tk, tk)
    sc = lax.dot_general(qs, kw_ref[0, pl.ds(wk0, n_win * tk), :], nt, preferred_element_type=F32)
    key = wk0 + lax.broadcasted_iota(jnp.int32, (tq, n_win * tk), 1)
    t_abs = q0 + lax.broadcasted_iota(jnp.int32, (tq, n_win * tk), 0)
    win_bias = jnp.where((key <= t_abs) & (key > t_abs - WINDOW), 0.0, NEG_INF)
    m_win = None
    for i in range(n_win):
        part = sc[:, i * tk:(i + 1) * tk].reshape(nh, tq, tk) + win_bias[None, :, i * tk:(i + 1) * tk]
        part = part.reshape(rows, tk)
        sw_ref[i] = part
        for c in range(tk // LANES):
            piece = part[:, c * LANES:(c + 1) * LANES]
            m_win = piece if m_win is None else jnp.maximum(m_win, piece)
    n_pad = jnp.maximum(WINDOW - 1 - (q0 + lax.broadcasted_iota(jnp.int32, (tq, LANES), 0)), 0).astype(F32)
    n_pad = jnp.concatenate([n_pad] * nh, axis=0)
    m_win = jnp.broadcast_to(jnp.max(m_win, axis=-1, keepdims=True), (rows, LANES))
    m_win = jnp.where(n_pad > 0.0, jnp.maximum(m_win, 0.0), m_win)
    m_wide = jnp.concatenate([m_win] * (tk // LANES), axis=1)
    pr = jnp.concatenate([jnp.exp2(sw_ref[i] - m_wide).astype(BF16) for i in range(n_win)], axis=1)
    v_ext = jnp.concatenate([vw_ref[0, pl.ds(wk0, n_win * tk), :], jnp.ones((n_win * tk, LANES), BF16)], axis=1)
    acc_w = jnp.dot(pr, v_ext, preferred_element_type=F32)
    pad_term = n_pad * jnp.exp2(jnp.where(n_pad > 0.0, -m_win, 0.0))
    ow_ref[...] = acc_w[:, :HEAD_DIM] * (1.0 / (acc_w[:, HEAD_DIM:] + pad_term))

    s = lax.dot_general(qs, kc_ref[0, 0], nt, preferred_element_type=F32).reshape(nh, tq, N_CHUNK)
    n_idx = lax.broadcasted_iota(jnp.int32, (tq, N_CHUNK), 1)
    t_idx = q0 + lax.broadcasted_iota(jnp.int32, (tq, N_CHUNK), 0)
    vis = ((n_idx * CMP_STRIDE + (CMP_BLOCK - 1) <= t_idx) & (n_idx < N_CMP))[None]
    sm = jnp.where(vis, s, NEG_INF)
    mx = jnp.maximum(jnp.max(sm, axis=-1, keepdims=True), 0.5 * NEG_INF)
    e = jnp.exp2(sm - mx)
    p = e * (1.0 / jnp.maximum(jnp.sum(e, axis=-1, keepdims=True), 1e-30))
    o_cmp = jnp.dot(p.reshape(rows, N_CHUNK).astype(BF16), vc_ref[0, 0], preferred_element_type=F32)

    p_sum = p[0] + p[1] + p[2] + p[3]
    p_hi = p_sum.astype(BF16)
    rem = p_sum - p_hi.astype(F32)
    p_mid = rem.astype(BF16)
    p_lo = (rem - p_mid.astype(F32)).astype(BF16)
    ovl = ovl_ref[...]
    imp = (lax.dot_general(ovl, p_hi, nt, preferred_element_type=F32)
           + lax.dot_general(ovl, p_mid, nt, preferred_element_type=F32)
           + lax.dot_general(ovl, p_lo, nt, preferred_element_type=F32))

    j_idx = lax.broadcasted_iota(jnp.int32, (N_SLC, tq), 0)
    cur = jnp.right_shift(q0 + lax.broadcasted_iota(jnp.int32, (N_SLC, tq), 1), int(np.log2(SLC_BLOCK)))
    valid = j_idx <= cur
    forced = (j_idx == 0) | (j_idx == cur) | (j_idx == cur - 1)
    score = jnp.where(forced & valid, FORCED_SCORE, jnp.where(valid, imp, MASKED_SCORE))
    rank = jnp.zeros((N_SLC, tq), jnp.int32)
    for k in range(N_SLC):
        sk = score[k:k + 1, :]
        beats = (sk > score) | ((sk == score) & (j_idx > k))
        rank = rank + beats.astype(jnp.int32)
    unsel_t = jnp.where(rank < N_SELECT, 0.0, 1.0)
    unsel_t = jnp.concatenate([unsel_t, jnp.zeros((LANES - N_SLC, tq), F32)], axis=0)
    unsel = jnp.transpose(unsel_t).astype(BF16)

    q_slc = jnp.concatenate([qs, jnp.concatenate([unsel] * nh, axis=0)], axis=1)

    local_q = lax.broadcasted_iota(jnp.int32, (tq, tk), 0)
    local_k = lax.broadcasted_iota(jnp.int32, (tq, tk), 1)
    causal_bias = jnp.where(local_k <= local_q, 0.0, NEG_INF)

    def slc_pass1(n):
        k = jnp.concatenate([ks_ref[0, 0:n * tk, :], blk_ref[0:n * tk, :]], axis=1)
        sc = lax.dot_general(q_slc, k, nt, preferred_element_type=F32)
        m = None
        for i in range(n):
            part = sc[:, i * tk:(i + 1) * tk]
            if i == n - 1:
                part = (part.reshape(nh, tq, tk) + causal_bias[None]).reshape(rows, tk)
            s_ref[i] = part
            for c in range(tk // LANES):
                piece = part[:, c * LANES:(c + 1) * LANES]
                m = piece if m is None else jnp.maximum(m, piece)
        mx_ref[...] = jnp.broadcast_to(jnp.max(m, axis=-1, keepdims=True), (rows, LANES))

    def slc_pass2(n):
        v_ext = jnp.concatenate([vs_ref[0, 0:n * tk, :], jnp.ones((n * tk, LANES), BF16)], axis=1)
        m_wide = jnp.concatenate([mx_ref[...]] * (tk // LANES), axis=1)
        pr = jnp.concatenate([jnp.exp2(s_ref[i] - m_wide).astype(BF16) for i in range(n)], axis=1)
        acc_ref[...] = jnp.dot(pr, v_ext, preferred_element_type=F32)

    n_q_tiles = SEQ // tq
    for n_tiles in range(1, n_q_tiles + 1):
        pl.when(qi == n_tiles - 1)(functools.partial(slc_pass1, n_tiles))

    for n_tiles in range(1, n_q_tiles + 1):
        pl.when(qi == n_tiles - 1)(functools.partial(slc_pass2, n_tiles))

    acc = acc_ref[...]
    o_slc = acc[:, :HEAD_DIM] * (1.0 / acc[:, HEAD_DIM:])

    o_win = ow_ref[...]

    for r in range(nh):
        sl = slice(r * tq, (r + 1) * tq)
        c = r * N_NSA_BRANCHES
        o_r = g_ref[c] * o_cmp[sl] + g_ref[c + 1] * o_slc[sl] + g_ref[c + 2] * o_win[sl]
        o_ref[0, :, r * HEAD_DIM:(r + 1) * HEAD_DIM] = o_r.astype(o_ref.dtype)


def _attention(proj3, gate_logits3, k_cmp, v_cmp, overlap_t, block_cols, *, tq=ATTN_TILE):
    b = proj3.shape[0]
    gw = HEADS_PER_GROUP * HEAD_DIM
    kv = lambda off: pl.BlockSpec((1, SEQ, HEAD_DIM), lambda i, g, t: (i, 0, off // HEAD_DIM + g))
    cmp_spec = pl.BlockSpec((1, 1, N_CHUNK, HEAD_DIM), lambda i, g, t: (i, g, 0, 0))
    rows = HEADS_PER_GROUP * tq
    return pl.pallas_call(
        functools.partial(_attn_kernel, tq=tq),
        grid=(b, N_KV_GROUPS, SEQ // tq),
        in_specs=[
            pl.BlockSpec((1, tq, gw), lambda i, g, t: (i, t, OFF_Q // gw + g)),
            kv(OFF_KS), kv(OFF_VS), kv(OFF_KW), kv(OFF_VW),
            cmp_spec, cmp_spec,
            pl.BlockSpec((1, tq, LANES), lambda i, g, t: (i, t, 0)),
            pl.BlockSpec((N_SLC, N_CHUNK), lambda i, g, t: (0, 0)),
            pl.BlockSpec((SEQ, LANES), lambda i, g, t: (0, 0)),
        ],
        out_specs=pl.BlockSpec((1, tq, gw), lambda i, g, t: (i, t, g)),
        out_shape=jax.ShapeDtypeStruct((b, SEQ, D_Q), BF16),
        scratch_shapes=[
            pltpu.VMEM((SEQ // tq, rows, tq), F32),
            pltpu.VMEM((rows, LANES), F32),
            pltpu.VMEM((rows, 2 * HEAD_DIM), F32),
            pltpu.VMEM((WINDOW // tq + 1, rows, tq), F32),
            pltpu.VMEM((rows, HEAD_DIM), F32),
            pltpu.VMEM((HEADS_PER_GROUP * N_NSA_BRANCHES, tq, LANES), F32),
        ],
        compiler_params=_params("parallel", "parallel", "arbitrary"),
        name="nsa_attention",
    )(proj3, proj3, proj3, proj3, proj3, k_cmp, v_cmp, gate_logits3, overlap_t, block_cols)


def _conv_kernel(a_ref, b_ref, ah_ref, bh_ref, w_ref, bias_ref, lg_ref, lb_ref, o_ref, u_ref, y_ref, *, ts):
    nc = D_CONV // LANES
    first = pl.program_id(1) == 0
    u_main = a_ref[0].astype(F32) * jax.nn.sigmoid(b_ref[0].astype(F32))
    u_halo = ah_ref[0].astype(F32) * jax.nn.sigmoid(bh_ref[0].astype(F32))
    u_halo = jnp.where(first, 0.0, u_halo)
    for c in range(nc):
        u_ref[c, 0:HALO, :] = u_halo[:, c * LANES:(c + 1) * LANES]
        u_ref[c, HALO:HALO + ts, :] = u_main[:, c * LANES:(c + 1) * LANES]

    rc = 64
    base = HALO - (CONV_WIDTH - 1)

    def chunk_body(c, carry):
        for r0 in range(0, ts, rc):
            acc = jnp.zeros((rc, LANES), F32)
            for j in range(CONV_WIDTH):
                acc = acc + u_ref[c, pl.ds(base + r0 + j, rc), :] * w_ref[c, j:j + 1, :]
            y_ref[c, r0:r0 + rc, :] = acc
        return carry

    lax.fori_loop(0, nc, chunk_body, 0)

    y = jnp.concatenate([y_ref[c] for c in range(nc)], axis=1) + bias_ref[...]
    mu = jnp.mean(y, axis=-1, keepdims=True)
    d = y - mu
    var = jnp.mean(d * d, axis=-1, keepdims=True)
    z = d * lax.rsqrt(var + LN_EPS) * lg_ref[...] + lb_ref[...]
    o_ref[0] = (z * jax.nn.sigmoid(z)).astype(o_ref.dtype)


def _conformer_conv(proj3, w_chunks, b_dw, ln_g, ln_b, *, ts=256):
    b = proj3.shape[0]
    nc = D_CONV // LANES
    per = ts // HALO
    main = lambda off: pl.BlockSpec((1, ts, D_CONV), lambda i, t: (i, t, off // D_CONV))
    halo = lambda off: pl.BlockSpec((1, HALO, D_CONV), lambda i, t: (i, jnp.maximum(t * per - 1, 0), off // D_CONV))
    vec = pl.BlockSpec((1, D_CONV), lambda i, t: (0, 0))
    return pl.pallas_call(
        functools.partial(_conv_kernel, ts=ts),
        grid=(b, SEQ // ts),
        in_specs=[main(OFF_GLU_A), main(OFF_GLU_B), halo(OFF_GLU_A), halo(OFF_GLU_B),
                  pl.BlockSpec((nc, HALO, LANES), lambda i, t: (0, 0, 0)), vec, vec, vec],
        out_specs=pl.BlockSpec((1, ts, D_CONV), lambda i, t: (i, t, 0)),
        out_shape=jax.ShapeDtypeStruct((b, SEQ, D_CONV), BF16),
        scratch_shapes=[pltpu.VMEM((nc, HALO + ts, LANES), F32), pltpu.VMEM((nc, ts, LANES), F32)],
        compiler_params=_params("parallel", "arbitrary"),
        name="conformer_conv",
    )(proj3, proj3, proj3, proj3, w_chunks, b_dw.reshape(1, D_CONV), ln_g.reshape(1, D_CONV), ln_b.reshape(1, D_CONV))


def _overlap_t():
    cmp_start = np.arange(N_CHUNK) * CMP_STRIDE
    slc_start = np.arange(N_SLC) * SLC_BLOCK
    ov = ((cmp_start[None, :] < slc_start[:, None] + SLC_BLOCK) & (cmp_start[None, :] + CMP_BLOCK > slc_start[:, None])
          & (np.arange(N_CHUNK)[None, :] < N_CMP))
    return jnp.asarray(ov.astype(np.float32), dtype=BF16)


def _block_cols():
    own = (np.arange(SEQ)[:, None] // SLC_BLOCK) == np.arange(LANES)[None, :]
    return jnp.asarray(np.where(own, -MASK_BIG, 0.0).astype(np.float32), dtype=BF16)


def _layer(x, norm_mix_pre, w_in, pos_cmp_k, w_cmp_k1, w_cmp_k2, pos_cmp_v, w_cmp_v1, w_cmp_v2,
           w_dw, b_dw, ln_conv_g, ln_conv_b, w_conv_out, w_attn_out, w_out, norm_mix_post,
           norm_mlp_pre, w_up, w_down, norm_mlp_post):
    b, s, d = x.shape
    m = b * s
    x2 = x.reshape(m, d)
    half = CMP_STRIDE * HEAD_DIM

    w_in_t = jnp.swapaxes(w_in, 0, 1)
    col_scale = jnp.ones((D_MAIN,), F32).at[OFF_Q:OFF_Q + D_Q].set(HEAD_DIM ** -0.5 * LOG2_E)

    u, gate_logits = _rmsnorm_and_branch_gates(x2, norm_mix_pre, w_in_t, D_MAIN)
    proj = _matmul_nt(u, w_in_t, 0, D_MAIN, BF16, name="in_proj", col_scale=col_scale)
    proj3 = proj.reshape(b, s, D_MAIN)

    w_chunks = jnp.pad(w_dw.reshape(CONV_WIDTH, D_CONV), ((0, HALO - CONV_WIDTH), (0, 0)))
    w_chunks = w_chunks.reshape(HALO, D_CONV // LANES, LANES).transpose(1, 0, 2)
    gates_ab = _matmul_nt(u, w_in_t, D_MAIN + N_GATE, 2 * D_MODEL, BF16, name="in_proj_merge_gates")
    conv = _conformer_conv(proj3, w_chunks, b_dw, ln_conv_g, ln_conv_b)

    k_cmp, v_cmp = _compress(
        proj3, pos_cmp_k.reshape(2, half), pos_cmp_v.reshape(2, half),
        w_cmp_k1.reshape(2, half, CMP_HIDDEN), w_cmp_v1.reshape(2, half, CMP_HIDDEN), w_cmp_k2, w_cmp_v2)

    attn = _attention(proj3, gate_logits.reshape(b, s, LANES), k_cmp, v_cmp, _overlap_t(), _block_cols())

    merged = _merge(conv.reshape(m, D_CONV), attn.reshape(m, D_Q), w_conv_out, w_attn_out, gates_ab)
    z = _matmul(merged, w_out, BF16, name="out_proj")
    x1, h = _post_mix(x2, z, norm_mix_post, norm_mlp_pre)

    hidden = _matmul(h, w_up, BF16, name="mlp_up", relu2=True)
    y = _matmul_ktiled(hidden, w_down, BF16, bm=2048, bn=1024, bk=1024, name="mlp_down")
    return _post_mlp(x1, y, norm_mlp_post).reshape(b, s, d)


def kernel(x, norm_mix_pre, w_in, pos_cmp_k, w_cmp_k1, w_cmp_k2, pos_cmp_v, w_cmp_v1, w_cmp_v2, w_dw, b_dw,
           ln_conv_g, ln_conv_b, w_conv_out, w_attn_out, w_out, norm_mix_post, norm_mlp_pre, w_up, w_down,
           norm_mlp_post):
    for l in range(norm_mix_pre.shape[0]):
        x = _layer(x, norm_mix_pre[l], w_in[l], pos_cmp_k[l], w_cmp_k1[l], w_cmp_k2[l], pos_cmp_v[l], w_cmp_v1[l],
                   w_cmp_v2[l], w_dw[l], b_dw[l], ln_conv_g[l], ln_conv_b[l], w_conv_out[l], w_attn_out[l], w_out[l],
                   norm_mix_post[l], norm_mlp_pre[l], w_up[l], w_down[l], norm_mlp_post[l])
    return x
```
